```python
import math
import jax, jax.numpy as jnp
from jax import lax
import numpy as np

D_MODEL = 4096
BATCH = 4
SEQ = 4096
DEPTH = 4

N_MIXERS = 3
N_HEADS = 32
HEAD_DIM = D_MODEL // N_HEADS
LRU_WIDTH = (4 * D_MODEL // 3) // 256 * 256
LRU_BLOCKS = 16
LRU_BLOCK_WIDTH = LRU_WIDTH // LRU_BLOCKS
LRU_C = 8.0
CONV_WIDTH = 4
SB_QBLOCK = 128
MOBA_BLOCK = 256
MOBA_TOPK = 3
MOBA_QCHUNK = 16
D_FF = 4 * D_MODEL
REL_BUCKETS = 32
REL_MAX_DIST = 128
NORM_EPS = 1e-6
N_LRU_LAYERS = (DEPTH + 2) // 3
N_SB_LAYERS = (DEPTH + 1) // 3
N_MOBA_LAYERS = DEPTH // 3

kernel_name = "hybrid_rglru_stickbreak_moba_trunk"


def rmsnorm(x, g):
    x32 = x.astype(jnp.float32)
    y = x32 * lax.rsqrt(jnp.mean(x32 * x32, axis=-1, keepdims=True) + NORM_EPS)
    return (y * g.astype(jnp.float32)).astype(x.dtype)


def t5_bucket(dist):
    n = jnp.maximum(dist, 0)
    max_exact = REL_BUCKETS // 2
    n_f = jnp.maximum(n, 1).astype(jnp.float32)
    large = max_exact + (jnp.log(n_f / max_exact) / math.log(REL_MAX_DIST / max_exact)
                         * (REL_BUCKETS - max_exact)).astype(jnp.int32)
    large = jnp.minimum(large, REL_BUCKETS - 1)
    return jnp.where(n < max_exact, n, large)


def causal_depthwise_conv(x, w, b):
    S = x.shape[1]
    xp = jnp.pad(x, ((0, 0), (CONV_WIDTH - 1, 0), (0, 0)))
    y = b
    for k in range(CONV_WIDTH):
        y = y + xp[:, k:k + S] * w[k]
    return y


def rglru_block(h, w_in, b_in, conv_w, conv_b, w_ga, b_ga, w_gx, b_gx, lam, w_out, b_out):
    B, S, _ = h.shape
    u = h @ w_in + b_in
    gate_branch = jax.nn.gelu(u[..., :LRU_WIDTH], approximate=True)
    xb = causal_depthwise_conv(u[..., LRU_WIDTH:], conv_w, conv_b)
    xh = xb.reshape(B, S, LRU_BLOCKS, LRU_BLOCK_WIDTH)
    r = jax.nn.sigmoid((jnp.einsum('bsnc,ncd->bsnd', xh, w_ga).reshape(B, S, LRU_WIDTH) + b_ga).astype(jnp.float32))
    i = jax.nn.sigmoid((jnp.einsum('bsnc,ncd->bsnd', xh, w_gx).reshape(B, S, LRU_WIDTH) + b_gx).astype(jnp.float32))
    log_a = -LRU_C * r * jax.nn.softplus(-lam.astype(jnp.float32))
    a = jnp.exp(log_a)
    inp = jnp.sqrt(-jnp.expm1(2.0 * log_a)) * (i * xb.astype(jnp.float32))

    def combine(c1, c2):
        a1, b1 = c1
        a2, b2 = c2
        return a1 * a2, a2 * b1 + b2

    _, hs = lax.associative_scan(combine, (a, inp), axis=1)
    y = hs.astype(h.dtype) * gate_branch
    return y @ w_out + b_out


def split_heads(h, w_qkv):
    B, S, _ = h.shape
    qkv = (h @ w_qkv).reshape(B, S, 3, N_HEADS, HEAD_DIM)
    q = jnp.moveaxis(qkv[:, :, 0], 1, 2)
    k = jnp.moveaxis(qkv[:, :, 1], 1, 2)
    v = jnp.moveaxis(qkv[:, :, 2], 1, 2)
    return q, k, v


def merge_heads(o, w_o):
    B, H, S, hd = o.shape
    return jnp.moveaxis(o, 1, 2).reshape(B, S, H * hd) @ w_o


def stick_breaking_attention(q, k, v):
    S = q.shape[2]
    scale = HEAD_DIM ** -0.5
    outs = []
    for t0 in range(0, S, SB_QBLOCK):
        t1 = t0 + SB_QBLOCK
        z = jnp.einsum('bhqd,bhkd->bhqk', q[:, :, t0:t1], k[:, :, :t1]).astype(jnp.float32) * scale
        strict = jnp.arange(t1)[None, :] < jnp.arange(t0, t1)[:, None]
        log_beta = jax.nn.log_sigmoid(z)
        log_keep = jnp.where(strict, jax.nn.log_sigmoid(-z), 0.0)
        later = lax.cumsum(log_keep, axis=3, reverse=True) - log_keep
        w = jnp.where(strict, jnp.exp(log_beta + later), 0.0)
        outs.append(jnp.einsum('bhqk,bhkd->bhqd', w.astype(v.dtype), v[:, :, :t1]))
    return jnp.concatenate(outs, axis=2)


def moba_attention(q, k, v, rel_bias):
    B, H, S, hd = q.shape
    scale = HEAD_DIM ** -0.5
    nkb = -(-S // MOBA_BLOCK)
    pad = nkb * MOBA_BLOCK - S
    k_blocks = jnp.pad(k, ((0, 0), (0, 0), (0, pad), (0, 0))).reshape(B, H, nkb, MOBA_BLOCK, hd)
    v_blocks = jnp.pad(v, ((0, 0), (0, 0), (0, pad), (0, 0))).reshape(B, H, nkb, MOBA_BLOCK, hd)
    k_mean = jnp.mean(k_blocks.astype(jnp.float32), axis=3)
    gate = jnp.einsum('bhsd,bhnd->bhsn', q.astype(jnp.float32), k_mean)
    q_blk = jnp.arange(S) // MOBA_BLOCK
    past = jnp.arange(nkb)[None, :] < q_blk[:, None]
    gate = jnp.where(past, gate, -jnp.inf)
    n_sel = max(1, min(MOBA_TOPK, nkb - 1))
    _, sel = lax.top_k(gate, n_sel)
    sel_valid = sel < q_blk[:, None]

    n_chunks = S // MOBA_QCHUNK

    def to_chunks(a):
        a = a.reshape((B, H, n_chunks, MOBA_QCHUNK) + a.shape[3:])
        a = jnp.moveaxis(a, 2, 1)
        return a.reshape((B * n_chunks, H, MOBA_QCHUNK) + a.shape[4:])

    b_idx = jnp.repeat(jnp.arange(B), n_chunks)
    c_idx = jnp.tile(jnp.arange(n_chunks), B)
    bias_by_head = rel_bias.T.astype(jnp.float32)
    offs = jnp.arange(MOBA_BLOCK)
    gather_blocks = jax.vmap(lambda blocks, idx: blocks[idx])
    lookup_bias = jax.vmap(lambda tab, bk: tab[bk])

    def chunk_fn(args):
        q_c, sel_c, valid_c, bi, ci = args
        kb = k_blocks[bi]
        vb = v_blocks[bi]
        t0 = ci * MOBA_QCHUNK
        t = t0 + jnp.arange(MOBA_QCHUNK)
        own = t0 // MOBA_BLOCK
        k_sel = gather_blocks(kb, sel_c)
        v_sel = gather_blocks(vb, sel_c)
        k_own = lax.dynamic_index_in_dim(kb, own, axis=1, keepdims=False)
        v_own = lax.dynamic_index_in_dim(vb, own, axis=1, keepdims=False)
        l_sel = jnp.einsum('hqd,hqnpd->hqnp', q_c, k_sel).astype(jnp.float32) * scale
        l_own = jnp.einsum('hqd,hpd->hqp', q_c, k_own).astype(jnp.float32) * scale
        d_sel = t[None, :, None, None] - (sel_c[..., None] * MOBA_BLOCK + offs)
        d_own = t[:, None] - (own * MOBA_BLOCK + offs)[None, :]
        l_sel = l_sel + lookup_bias(bias_by_head, t5_bucket(d_sel))
        l_own = l_own + bias_by_head[:, t5_bucket(d_own)]
        l_sel = jnp.where(valid_c[..., None], l_sel, -jnp.inf)
        l_own = jnp.where((d_own >= 0)[None], l_own, -jnp.inf)
        logits = jnp.concatenate([l_sel.reshape(H, MOBA_QCHUNK, n_sel * MOBA_BLOCK), l_own], axis=-1)
        p = jax.nn.softmax(logits, axis=-1)
        p_sel = p[..., :n_sel * MOBA_BLOCK].reshape(H, MOBA_QCHUNK, n_sel, MOBA_BLOCK).astype(v.dtype)
        p_own = p[..., n_sel * MOBA_BLOCK:].astype(v.dtype)
        return (jnp.einsum('hqnp,hqnpd->hqd', p_sel, v_sel)
                + jnp.einsum('hqp,hpd->hqd', p_own, v_own))

    out = lax.map(chunk_fn, (to_chunks(q), to_chunks(sel), to_chunks(sel_valid), b_idx, c_idx))
    out = jnp.moveaxis(out.reshape(B, n_chunks, H, MOBA_QCHUNK, hd), 1, 2)
    return out.reshape(B, H, S, hd)


def squared_relu_mlp(h, w_up, w_down):
    u = jax.nn.relu(h @ w_up)
    return (u * u) @ w_down


def setup_inputs(seed: int = 0) -> dict:
    key = jax.random.key(seed)
    ks = jax.random.split(key, 26)
    f32 = jnp.float32

    def nrm(k, shape, fan_in):
        return jax.random.normal(k, shape, f32) * (fan_in ** -0.5)

    def small(k, shape, s=0.01):
        return jax.random.normal(k, shape, f32) * s

    def gain(k):
        return 1.0 + 0.05 * jax.random.normal(k, (DEPTH, D_MODEL), f32)

    u = jax.random.uniform(ks[14], (N_LRU_LAYERS, LRU_WIDTH), f32, minval=0.9, maxval=0.999)
    a0 = u ** (1.0 / LRU_C)
    lam = jnp.log(a0) - jnp.log1p(-a0)
    return {
        "x": jax.random.normal(ks[0], (BATCH, SEQ, D_MODEL), f32),
        "norm_mix_pre": gain(ks[1]),
        "norm_mix_post": gain(ks[2]),
        "norm_ffn_pre": gain(ks[3]),
        "norm_ffn_post": gain(ks[4]),
        "lru_w_in": nrm(ks[5], (N_LRU_LAYERS, D_MODEL, 2 * LRU_WIDTH), D_MODEL),
        "lru_b_in": small(ks[6], (N_LRU_LAYERS, 2 * LRU_WIDTH)),
        "lru_conv_w": nrm(ks[7], (N_LRU_LAYERS, CONV_WIDTH, LRU_WIDTH), CONV_WIDTH),
        "lru_conv_b": small(ks[8], (N_LRU_LAYERS, LRU_WIDTH)),
        "lru_w_ga": nrm(ks[9], (N_LRU_LAYERS, LRU_BLOCKS, LRU_BLOCK_WIDTH, LRU_BLOCK_WIDTH), LRU_BLOCK_WIDTH),
        "lru_b_ga": small(ks[10], (N_LRU_LAYERS, LRU_WIDTH)),
        "lru_w_gx": nrm(ks[11], (N_LRU_LAYERS, LRU_BLOCKS, LRU_BLOCK_WIDTH, LRU_BLOCK_WIDTH), LRU_BLOCK_WIDTH),
        "lru_b_gx": small(ks[12], (N_LRU_LAYERS, LRU_WIDTH)),
        "lru_lambda": lam,
        "lru_w_out": nrm(ks[15], (N_LRU_LAYERS, LRU_WIDTH, D_MODEL), LRU_WIDTH),
        "lru_b_out": small(ks[16], (N_LRU_LAYERS, D_MODEL)),
        "sb_w_qkv": nrm(ks[17], (N_SB_LAYERS, D_MODEL, 3 * N_HEADS * HEAD_DIM), D_MODEL),
        "sb_w_o": nrm(ks[18], (N_SB_LAYERS, N_HEADS * HEAD_DIM, D_MODEL), N_HEADS * HEAD_DIM),
        "moba_w_qkv": nrm(ks[19], (N_MOBA_LAYERS, D_MODEL, 3 * N_HEADS * HEAD_DIM), D_MODEL),
        "moba_w_o": nrm(ks[20], (N_MOBA_LAYERS, N_HEADS * HEAD_DIM, D_MODEL), N_HEADS * HEAD_DIM),
        "rel_bias": small(ks[21], (REL_BUCKETS, N_HEADS), 0.5),
        "mlp_w_up": nrm(ks[22], (DEPTH, D_MODEL, D_FF), D_MODEL),
        "mlp_w_down": nrm(ks[23], (DEPTH, D_FF, D_MODEL), D_FF),
    }


def reference(x, norm_mix_pre, norm_mix_post, norm_ffn_pre, norm_ffn_post,
              lru_w_in, lru_b_in, lru_conv_w, lru_conv_b, lru_w_ga, lru_b_ga,
              lru_w_gx, lru_b_gx, lru_lambda, lru_w_out, lru_b_out,
              sb_w_qkv, sb_w_o, moba_w_qkv, moba_w_o, rel_bias,
              mlp_w_up, mlp_w_down):
    for layer in range(DEPTH):
        mixer = layer % N_MIXERS
        j = layer // N_MIXERS
        h = rmsnorm(x, norm_mix_pre[layer])
        if mixer == 0:
            m = rglru_block(h, lru_w_in[j], lru_b_in[j], lru_conv_w[j], lru_conv_b[j],
                            lru_w_ga[j], lru_b_ga[j], lru_w_gx[j], lru_b_gx[j],
                            lru_lambda[j], lru_w_out[j], lru_b_out[j])
        elif mixer == 1:
            q, k, v = split_heads(h, sb_w_qkv[j])
            m = merge_heads(stick_breaking_attention(q, k, v), sb_w_o[j])
        else:
            q, k, v = split_heads(h, moba_w_qkv[j])
            m = merge_heads(moba_attention(q, k, v, rel_bias), moba_w_o[j])
        x = x + rmsnorm(m, norm_mix_post[layer])
        h = rmsnorm(x, norm_ffn_pre[layer])
        x = x + rmsnorm(squared_relu_mlp(h, mlp_w_up[layer], mlp_w_down[layer]), norm_ffn_post[layer])
    return x
```

```python
import functools
import math

import jax
import jax.numpy as jnp
from jax import lax
from jax.experimental import pallas as pl
from jax.experimental.pallas import tpu as pltpu

HEAD_DIM = 128
LANES = 128
SUBLANES = 8
VMEM_LIMIT_BYTES = 56 * 1024 * 1024
NORM_EPS = 1e-6
LRU_C = 8.0
MOBA_BLOCK = 256
MOBA_TOPK = 3
REL_MAX_DIST = 128
NEG_BIG = -1e30

BF16 = jnp.bfloat16
F32 = jnp.float32


def _pick(dim, target, align):
    best = None
    for t in range(align, min(dim, target) + 1, align):
        if dim % t == 0:
            best = t
    assert best is not None, (dim, target, align)
    return best


def _params(*sem):
    return pltpu.CompilerParams(dimension_semantics=sem, vmem_limit_bytes=VMEM_LIMIT_BYTES)


def _rms_scale(v):
    return lax.rsqrt(jnp.mean(v * v, axis=-1, keepdims=True) + NORM_EPS)


def _rmsnorm_kernel(x_ref, g_ref, h_ref):
    x = x_ref[...]
    h_ref[...] = (x * _rms_scale(x) * g_ref[...]).astype(h_ref.dtype)


def rmsnorm(x, g):
    n, d = x.shape
    tr = _pick(n, 256, SUBLANES)
    return pl.pallas_call(
        _rmsnorm_kernel,
        out_shape=jax.ShapeDtypeStruct((n, d), BF16),
        grid=(n // tr,),
        in_specs=[pl.BlockSpec((tr, d), lambda i: (i, 0)),
                  pl.BlockSpec((1, d), lambda i: (0, 0))],
        out_specs=pl.BlockSpec((tr, d), lambda i: (i, 0)),
        compiler_params=_params("parallel"),
        name="rmsnorm",
    )(x, g.reshape(1, d))


def _resid_norm_kernel(x_ref, m_ref, gpost_ref, gnext_ref, xo_ref, ho_ref):
    m = m_ref[...]
    xn = x_ref[...] + m * _rms_scale(m) * gpost_ref[...]
    xo_ref[...] = xn
    ho_ref[...] = (xn * _rms_scale(xn) * gnext_ref[...]).astype(ho_ref.dtype)


def _resid_kernel(x_ref, m_ref, gpost_ref, xo_ref):
    m = m_ref[...]
    xo_ref[...] = x_ref[...] + m * _rms_scale(m) * gpost_ref[...]


def resid_norm(x, m, g_post, g_next):
    n, d = x.shape
    tr = _pick(n, 256, SUBLANES)
    row = pl.BlockSpec((tr, d), lambda i: (i, 0))
    vec = pl.BlockSpec((1, d), lambda i: (0, 0))
    if g_next is None:
        return pl.pallas_call(
            _resid_kernel,
            out_shape=jax.ShapeDtypeStruct((n, d), F32),
            grid=(n // tr,), in_specs=[row, row, vec], out_specs=row,
            compiler_params=_params("parallel"), name="resid",
        )(x, m, g_post.reshape(1, d)), None
    return pl.pallas_call(
        _resid_norm_kernel,
        out_shape=(jax.ShapeDtypeStruct((n, d), F32), jax.ShapeDtypeStruct((n, d), BF16)),
        grid=(n // tr,), in_specs=[row, row, vec, vec], out_specs=(row, row),
        compiler_params=_params("parallel"), name="resid_norm",
    )(x, m, g_post.reshape(1, d), g_next.reshape(1, d))


def _epilogue(r, bias_ref, act):
    if bias_ref is not None:
        r = r + bias_ref[...]
    if act == "relu2":
        r = jnp.maximum(r, 0.0)
        r = r * r
    return r


def _mm_kernel(*refs, nk, has_bias, act):
    a_ref, b_ref = refs[0], refs[1]
    bias_ref = refs[2] if has_bias else None
    o_ref = refs[2 + has_bias]
    part = jnp.dot(a_ref[...], b_ref[...], preferred_element_type=F32)
    if nk == 1:
        o_ref[...] = _epilogue(part, bias_ref, act).astype(o_ref.dtype)
        return
    acc_ref = refs[3 + has_bias]
    k = pl.program_id(2)

    @pl.when(k == 0)
    def _():
        acc_ref[...] = part

    @pl.when((k > 0) & (k < nk - 1))
    def _():
        acc_ref[...] += part

    @pl.when(k == nk - 1)
    def _():
        o_ref[...] = _epilogue(acc_ref[...] + part, bias_ref, act).astype(o_ref.dtype)


def matmul(a, b, bias=None, act=None, out_dtype=F32):
    m, kdim = a.shape
    _, n = b.shape
    tm = _pick(m, 1024, SUBLANES)
    tn = _pick(n, 1024, LANES)
    tk = _pick(kdim, 4096, LANES)
    nk = kdim // tk
    in_specs = [pl.BlockSpec((tm, tk), lambda i, j, k: (i, k)),
                pl.BlockSpec((tk, tn), lambda i, j, k: (k, j))]
    args = [a, b]
    if bias is not None:
        in_specs.append(pl.BlockSpec((1, tn), lambda i, j, k: (0, j)))
        args.append(bias.reshape(1, n).astype(F32))
    return pl.pallas_call(
        functools.partial(_mm_kernel, nk=nk, has_bias=bias is not None, act=act),
        out_shape=jax.ShapeDtypeStruct((m, n), out_dtype),
        grid=(m // tm, n // tn, nk),
        in_specs=in_specs,
        out_specs=pl.BlockSpec((tm, tn), lambda i, j, k: (i, j)),
        scratch_shapes=[pltpu.VMEM((tm, tn), F32)] if nk > 1 else [],
        compiler_params=_params("parallel", "parallel", "arbitrary"),
        name="matmul",
    )(*args)


def _softplus(x):
    return jnp.maximum(x, 0.0) + jnp.log1p(jnp.exp(-jnp.abs(x)))


def _neg_expm1(x, exp_x):
    series = -x * (1.0 + x * (1.0 / 2) * (1.0 + x * (1.0 / 3) * (1.0 + x * (1.0 / 4) * (1.0 + x * (1.0 / 5)))))
    return jnp.where(x > -0.0625, series, 1.0 - exp_x)


def _lru_kernel(g_ref, x_ref, cw_ref, cb_ref, wa_ref, ba_ref, wx_ref, bx_ref, lam_ref,
                y_ref, ext_ref, a_ref, hs_ref, hcar_ref, *, ts, conv_width):
    pad = SUBLANES

    @pl.when(pl.program_id(2) == 0)
    def _():
        ext_ref[0:pad, :] = jnp.zeros((pad, ext_ref.shape[1]), F32)
        hcar_ref[...] = jnp.zeros_like(hcar_ref)

    xcur = x_ref[...]
    ext_ref[pad:pad + ts, :] = xcur
    xb = cb_ref[...] + cw_ref[conv_width - 1:conv_width, :] * xcur
    for k in range(conv_width - 1):
        off = pad - (conv_width - 1) + k
        xb = xb + cw_ref[k:k + 1, :] * ext_ref[off:off + ts, :]
    ext_ref[0:pad, :] = xcur[ts - pad:ts, :]

    xb16 = xb.astype(BF16)
    r = jax.nn.sigmoid(jnp.dot(xb16, wa_ref[0], preferred_element_type=F32) + ba_ref[...])
    i = jax.nn.sigmoid(jnp.dot(xb16, wx_ref[0], preferred_element_type=F32) + bx_ref[...])
    log_a = (-LRU_C * r) * _softplus(-lam_ref[...])
    a = jnp.exp(log_a)
    a_ref[...] = a
    hs_ref[...] = jnp.sqrt(_neg_expm1(2.0 * log_a, a * a)) * (i * xb)

    def step(t, h):
        h = a_ref[pl.ds(t, 1), :] * h + hs_ref[pl.ds(t, 1), :]
        hs_ref[pl.ds(t, 1), :] = h
        return h

    hcar_ref[0:1, :] = lax.fori_loop(0, ts, step, hcar_ref[0:1, :], unroll=8)
    y_ref[...] = (hs_ref[...] * jax.nn.gelu(g_ref[...], approximate=True)).astype(y_ref.dtype)


def lru_core(u, conv_w, conv_b, wa, ba, wx, bx, lam, batch, seq):
    n, two_rp = u.shape
    rp = two_rp // 2
    nblk, bwp, _ = wa.shape
    ts = _pick(seq, 512, SUBLANES)
    nts = seq // ts
    conv_width = conv_w.shape[0]
    assert conv_width - 1 <= SUBLANES and nblk * bwp == rp
    tok = lambda b, c, s: (b * nts + s, c)
    vec = pl.BlockSpec((1, bwp), lambda b, c, s: (0, c))
    wblk = pl.BlockSpec((1, bwp, bwp), lambda b, c, s: (c, 0, 0))
    return pl.pallas_call(
        functools.partial(_lru_kernel, ts=ts, conv_width=conv_width),
        out_shape=jax.ShapeDtypeStruct((n, rp), BF16),
        grid=(batch, nblk, nts),
        in_specs=[pl.BlockSpec((ts, bwp), tok),
                  pl.BlockSpec((ts, bwp), lambda b, c, s: (b * nts + s, nblk + c)),
                  pl.BlockSpec((conv_width, bwp), lambda b, c, s: (0, c)),
                  vec, wblk, vec, wblk, vec, vec],
        out_specs=pl.BlockSpec((ts, bwp), tok),
        scratch_shapes=[pltpu.VMEM((ts + SUBLANES, bwp), F32),
                        pltpu.VMEM((ts, bwp), F32),
                        pltpu.VMEM((ts, bwp), F32),
                        pltpu.VMEM((SUBLANES, bwp), F32)],
        compiler_params=_params("parallel", "parallel", "arbitrary"),
        name="lru_core",
    )(u, u, conv_w, conv_b, wa, ba, wx, bx, lam)


def _pad_blocks(w, axis, nblk, bwp):
    shape = w.shape
    bw = shape[axis] // nblk
    w = w.reshape(shape[:axis] + (nblk, bw) + shape[axis + 1:])
    padw = [(0, 0)] * w.ndim
    padw[axis + 1] = (0, bwp - bw)
    w = jnp.pad(w, padw)
    return w.reshape(shape[:axis] + (nblk * bwp,) + shape[axis + 1:])


def rglru_block(h, w_in, b_in, conv_w, conv_b, w_ga, b_ga, w_gx, b_gx, lam, w_out, b_out,
                batch, seq):
    r = lam.shape[0]
    nblk, bw, _ = w_ga.shape
    bwp = -(-bw // LANES) * LANES
    padc = lambda w, axis: _pad_blocks(w, axis, nblk, bwp)
    w_in_p = jnp.concatenate([padc(w_in[:, :r], 1), padc(w_in[:, r:], 1)], axis=1).astype(BF16)
    b_in_p = jnp.concatenate([padc(b_in[:r], 0), padc(b_in[r:], 0)])
    padg = lambda w: jnp.pad(w, ((0, 0), (0, bwp - bw), (0, bwp - bw))).astype(BF16)
    row = lambda v: padc(v, 0).reshape(1, nblk * bwp)
    u = matmul(h, w_in_p, bias=b_in_p)
    y = lru_core(u, padc(conv_w, 1), row(conv_b), padg(w_ga), row(b_ga), padg(w_gx), row(b_gx),
                 row(lam), batch, seq)
    return matmul(y, padc(w_out, 0).astype(BF16), bias=b_out)


def _qk(q, k):
    return lax.dot_general(q, k, (((1,), (1,)), ((), ())), preferred_element_type=F32)


def _sb_kernel(q_ref, k_ref, v_ref, o_ref, *, tq):
    i = pl.program_id(2)
    q = q_ref[...]
    scale = HEAD_DIM ** -0.5
    row = lax.broadcasted_iota(jnp.int32, (tq, tq), 0)
    col = lax.broadcasted_iota(jnp.int32, (tq, tq), 1)
    strict = col < row
    upper = (row > col).astype(BF16)

    def block(kb, masked, carry):
        later_blocks, acc = carry
        start = pl.multiple_of(kb * tq, tq)
        k = k_ref[pl.ds(start, tq), :]
        v = v_ref[pl.ds(start, tq), :]
        z = _qk(q, k) * scale
        sp = jnp.log1p(jnp.exp(-jnp.abs(z)))
        log_beta = jnp.minimum(z, 0.0) - sp
        log_keep = -jnp.maximum(z, 0.0) - sp
        if masked:
            log_keep = jnp.where(strict, log_keep, 0.0)
        hi = log_keep.astype(BF16)
        lo = (log_keep - hi.astype(F32)).astype(BF16)
        later = (jnp.dot(hi, upper, preferred_element_type=F32)
                 + jnp.dot(lo, upper, preferred_element_type=F32) + later_blocks)
        w = jnp.exp(log_beta + later)
        if masked:
            w = jnp.where(strict, w, 0.0)
        acc = acc + jnp.dot(w.astype(BF16), v, preferred_element_type=F32)
        later_blocks = later_blocks + jnp.sum(log_keep, axis=1, keepdims=True)
        return later_blocks, acc

    carry = (jnp.zeros((tq, 1), F32), jnp.zeros((tq, HEAD_DIM), F32))
    carry = block(i, True, carry)
    carry = lax.fori_loop(0, i, lambda j, c: block(i - 1 - j, False, c), carry)
    o_ref[...] = carry[1].astype(o_ref.dtype)


def _attn_specs(n_heads, nq, tq, seq):
    q_spec = pl.BlockSpec((tq, HEAD_DIM), lambda b, h, i: (b * nq + i, h))
    k_spec = pl.BlockSpec((seq, HEAD_DIM), lambda b, h, i: (b, n_heads + h))
    v_spec = pl.BlockSpec((seq, HEAD_DIM), lambda b, h, i: (b, 2 * n_heads + h))
    return q_spec, k_spec, v_spec


def sb_attention(qkv, batch, seq):
    n, three_d = qkv.shape
    n_heads = three_d // (3 * HEAD_DIM)
    tq = _pick(seq, 256, LANES)
    nq = seq // tq
    q_spec, k_spec, v_spec = _attn_specs(n_heads, nq, tq, seq)
    return pl.pallas_call(
        functools.partial(_sb_kernel, tq=tq),
        out_shape=jax.ShapeDtypeStruct((n, n_heads * HEAD_DIM), BF16),
        grid=(batch, n_heads, nq),
        in_specs=[q_spec, k_spec, v_spec],
        out_specs=q_spec,
        compiler_params=_params("parallel", "parallel", "arbitrary"),
        name="sb_attention",
    )(qkv, qkv, qkv)


def _t5_bucket(dist, n_buckets):
    max_exact = n_buckets // 2
    n_f = jnp.maximum(dist, 1).astype(F32)
    large = max_exact + (jnp.log(n_f / max_exact) / math.log(REL_MAX_DIST / max_exact)
                         * (n_buckets - max_exact)).astype(jnp.int32)
    large = jnp.minimum(large, n_buckets - 1)
    return jnp.where(dist < max_exact, dist, large)


def _moba_kernel(bias_ref, q_ref, k_ref, v_ref, o_ref,
                 kmhi_ref, kmlo_ref, tile_ref, m_ref, l_ref, acc_ref, *, seq, n_buckets, nkp):
    blk = MOBA_BLOCK
    h = pl.program_id(1)
    i = pl.program_id(2)
    nkb = seq // blk
    scale = HEAD_DIM ** -0.5
    far_bias = bias_ref[n_buckets - 1, h]

    @pl.when(i == 0)
    def _():
        r = lax.broadcasted_iota(jnp.int32, (nkp, seq), 0)
        c = lax.broadcasted_iota(jnp.int32, (nkp, seq), 1)
        lo_edge = r * blk
        avg = jnp.where((c >= lo_edge) & (c < lo_edge + blk), 1.0 / blk, 0.0).astype(BF16)
        km = jnp.dot(avg, k_ref[...], preferred_element_type=F32)
        hi = km.astype(BF16)
        kmhi_ref[...] = hi
        kmlo_ref[...] = (km - hi.astype(F32)).astype(BF16)
        mcol = lax.broadcasted_iota(jnp.int32, (SUBLANES, 2 * blk), 1)
        dist = jnp.where(mcol <= blk, blk - mcol, 3 * blk - mcol)
        bucket = _t5_bucket(dist, n_buckets)
        vals = jnp.zeros((SUBLANES, 2 * blk), F32)
        for bk in range(n_buckets):
            vals = jnp.where(bucket == bk, bias_ref[bk, h], vals)
        base = jnp.broadcast_to(vals[0:1, :], (blk, 2 * blk))
        tile_ref[...] = pltpu.roll(base, 0, 1, stride=1, stride_axis=0)

    q = q_ref[...]
    gate = _qk(q, kmhi_ref[...]) + _qk(q, kmlo_ref[...])
    lane = lax.broadcasted_iota(jnp.int32, (blk, nkp), 1)
    lane_f = lane.astype(F32)
    neg_inf = jnp.float32(-jnp.inf)
    g = jnp.where(lane < i, gate, neg_inf)
    sel = jnp.zeros((blk, nkp), F32)
    for _ in range(max(1, min(MOBA_TOPK, nkb - 1))):
        mx = jnp.max(g, axis=1, keepdims=True)
        first = jnp.min(jnp.where(g == mx, lane_f, float(nkp)), axis=1, keepdims=True)
        pick = (lane_f == first) & (mx > neg_inf)
        sel = jnp.where(pick, 1.0, sel)
        g = jnp.where(pick, neg_inf, g)

    def attend(s, mask, v, first):
        s = jnp.where(mask, s, NEG_BIG)
        m_old = m_ref[...]
        m_new = jnp.max(s, axis=1, keepdims=True) if first else jnp.maximum(
            m_old, jnp.max(s, axis=1, keepdims=True))
        p = jnp.where(mask, jnp.exp(s - m_new), 0.0)
        pv = jnp.dot(p.astype(BF16), v, preferred_element_type=F32)
        if first:
            l_ref[...] = jnp.sum(p, axis=1, keepdims=True)
            acc_ref[...] = pv
        else:
            alpha = jnp.exp(m_old - m_new)
            l_ref[...] = alpha * l_ref[...] + jnp.sum(p, axis=1, keepdims=True)
            acc_ref[...] = alpha * acc_ref[...] + pv
        m_ref[...] = m_new

    def keys(n):
        start = pl.multiple_of(n * blk, blk)
        return k_ref[pl.ds(start, blk), :], v_ref[pl.ds(start, blk), :]

    row = lax.broadcasted_iota(jnp.int32, (blk, blk), 0)
    col = lax.broadcasted_iota(jnp.int32, (blk, blk), 1)
    k, v = keys(i)
    attend(_qk(q, k) * scale + tile_ref[:, blk:2 * blk], col <= row, v, True)

    def selected(n):
        return jnp.max(jnp.where(lane == n, sel, 0.0), axis=1, keepdims=True) > 0.0

    @pl.when(i >= 1)
    def _():
        k, v = keys(i - 1)
        mask = jnp.broadcast_to(selected(i - 1), (blk, blk))
        attend(_qk(q, k) * scale + tile_ref[:, 0:blk], mask, v, False)

    def far(n, carry):
        k, v = keys(n)
        mask = jnp.broadcast_to(selected(n), (blk, blk))
        attend(_qk(q, k) * scale + far_bias, mask, v, False)
        return carry

    lax.fori_loop(0, jnp.maximum(i - 1, 0), far, 0)
    o_ref[...] = (acc_ref[...] / l_ref[...]).astype(o_ref.dtype)


def moba_attention(qkv, rel_bias, batch, seq):
    n, three_d = qkv.shape
    n_heads = three_d // (3 * HEAD_DIM)
    blk = MOBA_BLOCK
    assert seq % blk == 0 and REL_MAX_DIST <= blk
    nq = seq // blk
    nkp = -(-nq // LANES) * LANES
    n_buckets = rel_bias.shape[0]
    q_spec, k_spec, v_spec = _attn_specs(n_heads, nq, blk, seq)
    return pl.pallas_call(
        functools.partial(_moba_kernel, seq=seq, n_buckets=n_buckets, nkp=nkp),
        out_shape=jax.ShapeDtypeStruct((n, n_heads * HEAD_DIM), BF16),
        grid=(batch, n_heads, nq),
        in_specs=[pl.BlockSpec(memory_space=pltpu.SMEM), q_spec, k_spec, v_spec],
        out_specs=q_spec,
        scratch_shapes=[pltpu.VMEM((nkp, HEAD_DIM), BF16),
                        pltpu.VMEM((nkp, HEAD_DIM), BF16),
                        pltpu.VMEM((blk, 2 * blk), F32),
                        pltpu.VMEM((blk, 1), F32),
                        pltpu.VMEM((blk, 1), F32),
                        pltpu.VMEM((blk, HEAD_DIM), F32)],
        compiler_params=_params("parallel", "parallel", "arbitrary"),
        name="moba_attention",
    )(rel_bias.astype(F32), qkv, qkv, qkv)


def kernel(x, norm_mix_pre, norm_mix_post, norm_ffn_pre, norm_ffn_post, lru_w_in, lru_b_in, lru_conv_w, lru_conv_b, lru_w_ga, lru_b_ga, lru_w_gx, lru_b_gx, lru_lambda, lru_w_out, lru_b_out, sb_w_qkv, sb_w_o, moba_w_qkv, moba_w_o, rel_bias, mlp_w_up, mlp_w_down):
    batch, seq, d = x.shape
    depth = norm_mix_pre.shape[0]
    n_mixers = 3
    xt = x.reshape(batch * seq, d)
    h = rmsnorm(xt, norm_mix_pre[0])
    for layer in range(depth):
        mixer = layer % n_mixers
        j = layer // n_mixers
        if mixer == 0:
            m = rglru_block(h, lru_w_in[j], lru_b_in[j], lru_conv_w[j], lru_conv_b[j],
                            lru_w_ga[j], lru_b_ga[j], lru_w_gx[j], lru_b_gx[j],
                            lru_lambda[j], lru_w_out[j], lru_b_out[j], batch, seq)
        elif mixer == 1:
            qkv = matmul(h, sb_w_qkv[j].astype(BF16), out_dtype=BF16)
            m = matmul(sb_attention(qkv, batch, seq), sb_w_o[j].astype(BF16))
        else:
            qkv = matmul(h, moba_w_qkv[j].astype(BF16), out_dtype=BF16)
            m = matmul(moba_attention(qkv, rel_bias, batch, seq), moba_w_o[j].astype(BF16))
        xt, h = resid_norm(xt, m, norm_mix_post[layer], norm_ffn_pre[layer])
        u = matmul(h, mlp_w_up[layer].astype(BF16), act="relu2", out_dtype=BF16)
        m = matmul(u, mlp_w_down[layer].astype(BF16))
        g_next = norm_mix_pre[layer + 1] if layer + 1 < depth else None
        xt, h = resid_norm(xt, m, norm_ffn_post[layer], g_next)
    return xt.reshape(batch, seq, d)
```

```python
import functools
import math

import jax
import jax.numpy as jnp
import numpy as np
from jax import lax
from jax.experimental import pallas as pl
from jax.experimental.pallas import tpu as pltpu

HEAD_DIM = 128
LANES = 128
SUBLANES = 8
BF16_ROWS = 16
VMEM_LIMIT_BYTES = 56 * 1024 * 1024
NORM_EPS = 1e-6
LRU_C = 8.0
LRU_CHANNEL_TILE = 2688
MOBA_BLOCK = 256
MOBA_TOPK = 3
REL_MAX_DIST = 128
NEG_BIG = -1e30
MASK_BIG = 2.0 ** 100
PIPE = 4

BF16 = jnp.bfloat16
F32 = jnp.float32


def _pick(dim, target, align):
    best = None
    for t in range(align, min(dim, target) + 1, align):
        if dim % t == 0:
            best = t
    assert best is not None, (dim, target, align)
    return best


def _params(*sem):
    return pltpu.CompilerParams(dimension_semantics=sem, vmem_limit_bytes=VMEM_LIMIT_BYTES)


def _rms_scale(v):
    return lax.rsqrt(jnp.mean(v * v, axis=-1, keepdims=True) + NORM_EPS)


def _rmsnorm_kernel(x_ref, g_ref, h_ref):
    x = x_ref[...]
    h_ref[...] = (x * _rms_scale(x) * g_ref[...]).astype(h_ref.dtype)


def rmsnorm(x, g):
    n, d = x.shape
    tr = _pick(n, 256, SUBLANES)
    return pl.pallas_call(
        _rmsnorm_kernel,
        out_shape=jax.ShapeDtypeStruct((n, d), BF16),
        grid=(n // tr,),
        in_specs=[pl.BlockSpec((tr, d), lambda i: (i, 0)),
                  pl.BlockSpec((1, d), lambda i: (0, 0))],
        out_specs=pl.BlockSpec((tr, d), lambda i: (i, 0)),
        compiler_params=_params("parallel"),
        name="rmsnorm",
    )(x, g.reshape(1, d))


def _resid_norm_kernel(x_ref, m_ref, gpost_ref, gnext_ref, xo_ref, ho_ref):
    m = m_ref[...]
    xn = x_ref[...] + m * _rms_scale(m) * gpost_ref[...]
    xo_ref[...] = xn
    ho_ref[...] = (xn * _rms_scale(xn) * gnext_ref[...]).astype(ho_ref.dtype)


def _resid_kernel(x_ref, m_ref, gpost_ref, xo_ref):
    m = m_ref[...]
    xo_ref[...] = x_ref[...] + m * _rms_scale(m) * gpost_ref[...]


def resid_norm(x, m, g_post, g_next):
    n, d = x.shape
    tr = _pick(n, 256, SUBLANES)
    row = pl.BlockSpec((tr, d), lambda i: (i, 0))
    vec = pl.BlockSpec((1, d), lambda i: (0, 0))
    if g_next is None:
        return pl.pallas_call(
            _resid_kernel,
            out_shape=jax.ShapeDtypeStruct((n, d), F32),
            grid=(n // tr,), in_specs=[row, row, vec], out_specs=row,
            compiler_params=_params("parallel"), name="resid",
        )(x, m, g_post.reshape(1, d)), None
    return pl.pallas_call(
        _resid_norm_kernel,
        out_shape=(jax.ShapeDtypeStruct((n, d), F32), jax.ShapeDtypeStruct((n, d), BF16)),
        grid=(n // tr,), in_specs=[row, row, vec, vec], out_specs=(row, row),
        compiler_params=_params("parallel"), name="resid_norm",
    )(x, m, g_post.reshape(1, d), g_next.reshape(1, d))


def _epilogue(r, bias_ref, act):
    if bias_ref is not None:
        r = r + bias_ref[...]
    if act == "relu2":
        r = jnp.maximum(r, 0.0)
        r = r * r
    return r


def _mm_kernel(*refs, nk, has_bias, act):
    a_ref, b_ref = refs[0], refs[1]
    bias_ref = refs[2] if has_bias else None
    o_ref = refs[2 + has_bias]
    part = jnp.dot(a_ref[...], b_ref[...], preferred_element_type=F32)
    if nk == 1:
        o_ref[...] = _epilogue(part, bias_ref, act).astype(o_ref.dtype)
        return
    acc_ref = refs[3 + has_bias]
    k = pl.program_id(2)

    @pl.when(k == 0)
    def _():
        acc_ref[...] = part

    @pl.when((k > 0) & (k < nk - 1))
    def _():
        acc_ref[...] += part

    @pl.when(k == nk - 1)
    def _():
        o_ref[...] = _epilogue(acc_ref[...] + part, bias_ref, act).astype(o_ref.dtype)


def matmul(a, b, bias=None, act=None, out_dtype=F32):
    m, kdim = a.shape
    _, n = b.shape
    tm = _pick(m, 1024, SUBLANES)
    tn = _pick(n, 1024, LANES)
    tk = _pick(kdim, 4096, LANES)
    nk = kdim // tk
    in_specs = [pl.BlockSpec((tm, tk), lambda i, j, k: (i, k)),
                pl.BlockSpec((tk, tn), lambda i, j, k: (k, j))]
    args = [a, b]
    if bias is not None:
        in_specs.append(pl.BlockSpec((1, tn), lambda i, j, k: (0, j)))
        args.append(bias.reshape(1, n).astype(F32))
    return pl.pallas_call(
        functools.partial(_mm_kernel, nk=nk, has_bias=bias is not None, act=act),
        out_shape=jax.ShapeDtypeStruct((m, n), out_dtype),
        grid=(m // tm, n // tn, nk),
        in_specs=in_specs,
        out_specs=pl.BlockSpec((tm, tn), lambda i, j, k: (i, j)),
        scratch_shapes=[pltpu.VMEM((tm, tn), F32)] if nk > 1 else [],
        compiler_params=_params("parallel", "parallel", "arbitrary"),
        name="matmul",
    )(*args)


def _softplus(x):
    return jnp.maximum(x, 0.0) + jnp.log1p(jnp.exp(-jnp.abs(x)))


def _neg_expm1(x, exp_x):
    series = (((x * (1.0 / 24) + (1.0 / 6)) * x + 0.5) * x + 1.0) * (-x)
    return jnp.where(x > -1.0 / 32, series, 1.0 - exp_x)


def _lru_kernel(g_ref, x_ref, cw_ref, cb_ref, wa_ref, ba_ref, wx_ref, bx_ref, lam_ref,
                y_ref, ext_ref, a_ref, hs_ref, hcar_ref, *, ts, conv_width, windows):
    pad = SUBLANES

    @pl.when(pl.program_id(2) == 0)
    def _():
        ext_ref[0:pad, :] = jnp.zeros((pad, ext_ref.shape[1]), F32)
        hcar_ref[...] = jnp.zeros_like(hcar_ref)

    xcur = x_ref[...]
    ext_ref[pad:pad + ts, :] = xcur
    xb = cb_ref[...] + cw_ref[conv_width - 1:conv_width, :] * xcur
    for k in range(conv_width - 1):
        off = pad - (conv_width - 1) + k
        xb = xb + cw_ref[k:k + 1, :] * ext_ref[off:off + ts, :]
    ext_ref[0:pad, :] = xcur[ts - pad:ts, :]

    xb16 = xb.astype(BF16)

    def gate(w_ref, b_ref, z_ref):
        z_ref[...] = jnp.broadcast_to(b_ref[...], z_ref.shape)
        for m, (a0, wd) in enumerate(windows):
            z_ref[:, a0:a0 + wd] += jnp.dot(xb16[:, a0:a0 + wd], w_ref[m, 0:wd, 0:wd],
                                            preferred_element_type=F32)
        return jax.nn.sigmoid(z_ref[...])

    r = gate(wa_ref, ba_ref, a_ref)
    i = gate(wx_ref, bx_ref, hs_ref)
    log_a = (-LRU_C * r) * _softplus(-lam_ref[...])
    a = jnp.exp(log_a)
    a_ref[...] = a
    var = _neg_expm1(2.0 * log_a, a * a)
    hs_ref[...] = (var * lax.rsqrt(jnp.maximum(var, 1e-30))) * (i * xb)

    def step(t, h):
        h = a_ref[pl.ds(t, 1), :] * h + hs_ref[pl.ds(t, 1), :]
        hs_ref[pl.ds(t, 1), :] = h
        return h

    hcar_ref[0:1, :] = lax.fori_loop(0, ts, step, hcar_ref[0:1, :], unroll=8)
    y_ref[...] = (hs_ref[...] * jax.nn.gelu(g_ref[...], approximate=True)).astype(y_ref.dtype)


def _gate_windows(bw, nbs):
    starts = [bw * m // LANES * LANES for m in range(nbs)]
    ends = [-(-bw * (m + 1) // LANES) * LANES for m in range(nbs)]
    windows = tuple((a0, b0 - a0) for a0, b0 in zip(starts, ends))
    offsets = [bw * m - a0 for m, a0 in enumerate(starts)]
    return windows, offsets


def _window_pad(w, offsets, wmax):
    nblk, bw, _ = w.shape
    nbs = len(offsets)
    parts = [jnp.pad(w[m::nbs], ((0, 0), (o, wmax - o - bw), (o, wmax - o - bw)))
             for m, o in enumerate(offsets)]
    return jnp.stack(parts, axis=1).reshape(nblk, wmax, wmax).astype(BF16)


def lru_core(u, conv_w, conv_b, w_ga, b_ga, w_gx, b_gx, lam, batch, seq):
    n, two_r = u.shape
    r = two_r // 2
    nblk, bw, _ = w_ga.shape
    ts = _pick(seq, 256, SUBLANES)
    nts = seq // ts
    cw = _pick(r, LRU_CHANNEL_TILE, bw * LANES // math.gcd(bw, LANES))
    nbs, ncw = cw // bw, r // cw
    windows, offsets = _gate_windows(bw, nbs)
    wmax = max(wd for _, wd in windows)
    conv_width = conv_w.shape[0]
    assert conv_width - 1 <= SUBLANES and nblk * bw == r
    tok = lambda b, c, s: (b * nts + s, c)
    vec = pl.BlockSpec((1, cw), lambda b, c, s: (0, c))
    wblk = pl.BlockSpec((nbs, wmax, wmax), lambda b, c, s: (c, 0, 0))
    row = lambda v: v.reshape(1, r)
    return pl.pallas_call(
        functools.partial(_lru_kernel, ts=ts, conv_width=conv_width, windows=windows),
        out_shape=jax.ShapeDtypeStruct((n, r), BF16),
        grid=(batch, ncw, nts),
        in_specs=[pl.BlockSpec((ts, cw), tok),
                  pl.BlockSpec((ts, cw), lambda b, c, s: (b * nts + s, ncw + c)),
                  pl.BlockSpec((conv_width, cw), lambda b, c, s: (0, c)),
                  vec, wblk, vec, wblk, vec, vec],
        out_specs=pl.BlockSpec((ts, cw), tok),
        scratch_shapes=[pltpu.VMEM((ts + SUBLANES, cw), F32),
                        pltpu.VMEM((ts, cw), F32),
                        pltpu.VMEM((ts, cw), F32),
                        pltpu.VMEM((SUBLANES, cw), F32)],
        compiler_params=_params("parallel", "parallel", "arbitrary"),
        name="lru_core",
    )(u, u, conv_w, row(conv_b), _window_pad(w_ga, offsets, wmax), row(b_ga),
      _window_pad(w_gx, offsets, wmax), row(b_gx), row(lam))


def rglru_block(h, w_in, b_in, conv_w, conv_b, w_ga, b_ga, w_gx, b_gx, lam, w_out, b_out,
                batch, seq):
    u = matmul(h, w_in.astype(BF16), bias=b_in)
    y = lru_core(u, conv_w, conv_b, w_ga, b_ga, w_gx, b_gx, lam, batch, seq)
    return matmul(y, w_out.astype(BF16), bias=b_out)


def _qk(q, k):
    return lax.dot_general(q, k, (((1,), (1,)), ((), ())), preferred_element_type=F32)


def _pipeline_iters(n_steps, lead):
    return -(-(n_steps + lead) // PIPE) * PIPE


def _step_table(rows, lead):
    n = len(rows)
    idx = np.clip(np.arange(_pipeline_iters(n, lead) + lead) - lead, 0, n - 1)
    return jnp.asarray(np.asarray(rows, np.int32)[idx].T)


def _sb_kernel(tab_ref, q_ref, k_ref, v_ref, o_ref, z_ref, e_ref, r_ref, acc_ref,
               upper_ref, keep_ref, pen_ref, *, tq, n_steps):
    scale = HEAD_DIM ** -0.5
    row = lax.broadcasted_iota(jnp.int32, (tq, tq), 0)
    col = lax.broadcasted_iota(jnp.int32, (tq, tq), 1)
    upper = (row > col).astype(BF16)
    upper_ref[0:tq, :] = upper
    upper_ref[tq:2 * tq, :] = upper
    strict = col < row
    keep_ref[0:tq, :] = jnp.ones((tq, tq), F32)
    keep_ref[tq:2 * tq, :] = jnp.where(strict, 1.0, 0.0)
    pen_ref[0:tq, :] = jnp.zeros((tq, tq), F32)
    pen_ref[tq:2 * tq, :] = jnp.where(strict, 0.0, NEG_BIG)
    z_ref[...] = jnp.zeros_like(z_ref)
    e_ref[...] = jnp.zeros_like(e_ref)
    r_ref[...] = jnp.zeros_like(r_ref)
    acc_ref[...] = jnp.zeros_like(acc_ref)

    def rows(t):
        return pl.ds(pl.multiple_of(t * tq, tq), tq)

    def iteration(it, par):
        qc, kc, first_c = tab_ref[0, it], tab_ref[1, it], tab_ref[2, it]
        w = jnp.exp(e_ref[par]).astype(BF16)
        acc = jnp.where(first_c == 1, 0.0, acc_ref[...]) + jnp.dot(
            w, v_ref[rows(kc), :], preferred_element_type=F32)
        acc_ref[...] = acc
        o_ref[rows(qc), :] = acc.astype(o_ref.dtype)
        first_b = tab_ref[2, it + PIPE]
        diag = pl.ds(pl.multiple_of(first_b * tq, tq), tq)
        z = z_ref[par]
        sp = jnp.log(1.0 + jnp.exp(-jnp.abs(z)))
        log_beta = jnp.minimum(z, 0.0) - sp
        log_keep = (log_beta - z) * keep_ref[diag, :]
        hi = log_keep.astype(BF16)
        lo = (log_keep - hi.astype(F32)).astype(BF16)
        later = jnp.dot(jnp.concatenate([hi, lo], axis=1), upper_ref[...],
                        preferred_element_type=F32)
        r_in = jnp.where(first_b == 1, 0.0, r_ref[...])
        e_ref[par] = log_beta + later + r_in + pen_ref[diag, :]
        r_ref[...] = r_in + jnp.sum(log_keep, axis=1, keepdims=True)
        qa, ka = tab_ref[0, it + 2 * PIPE], tab_ref[1, it + 2 * PIPE]
        z_ref[par] = _qk(q_ref[rows(qa), :], k_ref[rows(ka), :]) * scale

    def body(j, carry):
        for par in range(PIPE):
            iteration(PIPE * j + par, par)
        return carry

    lax.fori_loop(0, _pipeline_iters(n_steps, 2 * PIPE) // PIPE, body, 0)


def _attn_specs(n_heads, seq):
    q_spec = pl.BlockSpec((seq, HEAD_DIM), lambda b, h: (b, h))
    k_spec = pl.BlockSpec((seq, HEAD_DIM), lambda b, h: (b, n_heads + h))
    v_spec = pl.BlockSpec((seq, HEAD_DIM), lambda b, h: (b, 2 * n_heads + h))
    return q_spec, k_spec, v_spec


def sb_attention(qkv, batch, seq):
    n, three_d = qkv.shape
    n_heads = three_d // (3 * HEAD_DIM)
    tq = _pick(seq, 256, LANES)
    nq = seq // tq
    steps = [(i, kb, int(kb == i)) for i in range(nq) for kb in range(i, -1, -1)]
    steps.append(steps[0])
    q_spec, k_spec, v_spec = _attn_specs(n_heads, seq)
    return pl.pallas_call(
        functools.partial(_sb_kernel, tq=tq, n_steps=len(steps)),
        out_shape=jax.ShapeDtypeStruct((n, n_heads * HEAD_DIM), BF16),
        grid=(batch, n_heads),
        in_specs=[pl.BlockSpec(memory_space=pltpu.SMEM), q_spec, k_spec, v_spec],
        out_specs=q_spec,
        scratch_shapes=[pltpu.VMEM((PIPE, tq, tq), F32),
                        pltpu.VMEM((PIPE, tq, tq), F32),
                        pltpu.VMEM((tq, 1), F32),
                        pltpu.VMEM((tq, HEAD_DIM), F32),
                        pltpu.VMEM((2 * tq, tq), BF16),
                        pltpu.VMEM((2 * tq, tq), F32),
                        pltpu.VMEM((2 * tq, tq), F32)],
        compiler_params=_params("parallel", "parallel"),
        name="sb_attention",
    )(_step_table(steps, 2 * PIPE), qkv, qkv, qkv)


def _t5_bucket(dist, n_buckets):
    max_exact = n_buckets // 2
    n_f = jnp.maximum(dist, 1).astype(F32)
    large = max_exact + (jnp.log(n_f / max_exact) / math.log(REL_MAX_DIST / max_exact)
                         * (n_buckets - max_exact)).astype(jnp.int32)
    large = jnp.minimum(large, n_buckets - 1)
    return jnp.where(dist < max_exact, dist, large)


def _moba_kernel(tab_ref, bias_ref, q_ref, k_ref, v_ref, o_ref,
                 sel_ref, pos_ref, s_ref, m_ref, acc_ref, *, seq, n_buckets, nkp, n_steps):
    blk = MOBA_BLOCK
    h = pl.program_id(1)
    nkb = seq // blk
    nkr = -(-nkb // BF16_ROWS) * BF16_ROWS
    scale = HEAD_DIM ** -0.5

    r = lax.broadcasted_iota(jnp.int32, (nkr, seq), 0)
    c = lax.broadcasted_iota(jnp.int32, (nkr, seq), 1)
    lo_edge = r * blk
    avg = jnp.where((c >= lo_edge) & (c < lo_edge + blk), 1.0 / blk, 0.0).astype(BF16)
    km = jnp.dot(avg, k_ref[...], preferred_element_type=F32)
    km_hi = km.astype(BF16)
    km_lo = (km - km_hi.astype(F32)).astype(BF16)

    q_all = q_ref[...]
    gate = _qk(km_hi, q_all) + _qk(km_lo, q_all)
    neg_inf = jnp.float32(-jnp.inf)
    g = jnp.where(lo_edge + blk <= c, gate, neg_inf)
    r_f = r.astype(F32)
    sel = jnp.zeros((nkr, seq), F32)
    for _ in range(max(1, min(MOBA_TOPK, nkb - 1))):
        mx = jnp.max(g, axis=0, keepdims=True)
        first = jnp.min(jnp.where(g == mx, r_f, float(nkr)), axis=0, keepdims=True)
        pick = (r_f == first) & (mx > neg_inf)
        sel = jnp.where(pick, 1.0, sel)
        g = jnp.where(pick, neg_inf, g)
    sel_t = (sel - 1.0).astype(BF16)
    row = lax.broadcasted_iota(jnp.int32, (blk, blk), 0)
    col = lax.broadcasted_iota(jnp.int32, (blk, blk), 1)
    eye = (row == col).astype(BF16)
    pad = jnp.zeros((nkp - nkr, blk), BF16)
    for i in range(nkb):
        cols = jnp.concatenate([sel_t[:, i * blk:(i + 1) * blk], pad], axis=0)
        sel_ref[i * blk:(i + 1) * blk, :] = _qk(eye, cols).astype(BF16)

    mcol = lax.broadcasted_iota(jnp.int32, (SUBLANES, 2 * blk), 1)
    dist = jnp.where(mcol <= blk, blk - mcol, 3 * blk - mcol)
    bucket = _t5_bucket(dist, n_buckets)
    vals = jnp.zeros((SUBLANES, 2 * blk), F32)
    for bk in range(n_buckets):
        vals = jnp.where(bucket == bk, bias_ref[bk, h], vals)
    tile = pltpu.roll(jnp.broadcast_to(vals[0:1, :], (blk, 2 * blk)), 0, 1, stride=1, stride_axis=0)
    pos_ref[0:blk, :] = tile[:, 0:blk]
    pos_ref[blk:2 * blk, :] = jnp.where(col <= row, tile[:, blk:2 * blk], NEG_BIG)
    pos_ref[2 * blk:3 * blk, :] = jnp.full((blk, blk), bias_ref[n_buckets - 1, h], F32)

    def rows(t):
        return pl.ds(pl.multiple_of(t * blk, blk), blk)

    s_ref[...] = jnp.zeros_like(s_ref)
    m_ref[...] = jnp.zeros_like(m_ref)
    acc_ref[...] = jnp.zeros_like(acc_ref)
    ones = jnp.ones((blk, HEAD_DIM), BF16)
    key_lane = lax.broadcasted_iota(jnp.int32, (blk, nkp), 1)

    def iteration(it, par):
        qb, kb, own_b = tab_ref[0, it], tab_ref[1, it], tab_ref[2, it]
        s = s_ref[par]
        m_old = jnp.where(own_b == 1, NEG_BIG, m_ref[...])
        m_new = jnp.maximum(m_old, jnp.broadcast_to(jnp.max(s, axis=1, keepdims=True),
                                                    (blk, HEAD_DIM)))
        p = jnp.exp(s - jnp.concatenate([m_new, m_new], axis=1))
        alpha = jnp.exp(m_old - m_new)
        v_ext = jnp.concatenate([v_ref[rows(kb), :], ones], axis=1)
        acc = (jnp.concatenate([alpha, alpha], axis=1) * acc_ref[...]
               + jnp.dot(p.astype(BF16), v_ext, preferred_element_type=F32))
        m_ref[...] = m_new
        acc_ref[...] = acc
        o_ref[rows(qb), :] = (acc[:, 0:HEAD_DIM] / acc[:, HEAD_DIM:2 * HEAD_DIM]).astype(o_ref.dtype)
        qa, ka, pos_a, lane_a = (tab_ref[0, it + PIPE], tab_ref[1, it + PIPE],
                                 tab_ref[3, it + PIPE], tab_ref[4, it + PIPE])
        q_ext = jnp.concatenate([q_ref[rows(qa), :], sel_ref[rows(qa), :]], axis=1)
        k_ext = jnp.concatenate(
            [k_ref[rows(ka), :], jnp.where(key_lane == lane_a, MASK_BIG, 0.0).astype(BF16)], axis=1)
        pos = pos_ref[pl.ds(pl.multiple_of(pos_a, blk), blk), :]
        s_ref[par] = _qk(q_ext, k_ext) * scale + pos

    def body(j, carry):
        for par in range(PIPE):
            iteration(PIPE * j + par, par)
        return carry

    lax.fori_loop(0, _pipeline_iters(n_steps, PIPE) // PIPE, body, 0)


def moba_attention(qkv, rel_bias, batch, seq):
    n, three_d = qkv.shape
    n_heads = three_d // (3 * HEAD_DIM)
    blk = MOBA_BLOCK
    assert seq % blk == 0 and REL_MAX_DIST <= blk
    nq = seq // blk
    nkp = -(-nq // LANES) * LANES
    n_buckets = rel_bias.shape[0]
    assert nq < nkp
    steps = []
    for i in range(nq):
        steps.append((i, i, 1, blk, nkp - 1))
        if i >= 1:
            steps.append((i, i - 1, 0, 0, i - 1))
        steps.extend((i, kb, 0, 2 * blk, kb) for kb in range(i - 1))
    steps.append(steps[0])
    smem = pl.BlockSpec(memory_space=pltpu.SMEM)
    q_spec, k_spec, v_spec = _attn_specs(n_heads, seq)
    return pl.pallas_call(
        functools.partial(_moba_kernel, seq=seq, n_buckets=n_buckets, nkp=nkp, n_steps=len(steps)),
        out_shape=jax.ShapeDtypeStruct((n, n_heads * HEAD_DIM), BF16),
        grid=(batch, n_heads),
        in_specs=[smem, smem, q_spec, k_spec, v_spec],
        out_specs=q_spec,
        scratch_shapes=[pltpu.VMEM((seq, nkp), BF16),
                        pltpu.VMEM((3 * blk, blk), F32),
                        pltpu.VMEM((PIPE, blk, blk), F32),
                        pltpu.VMEM((blk, HEAD_DIM), F32),
                        pltpu.VMEM((blk, 2 * HEAD_DIM), F32)],
        compiler_params=_params("parallel", "parallel"),
        name="moba_attention",
    )(_step_table(steps, PIPE), rel_bias.astype(F32), qkv, qkv, qkv)


def kernel(x, norm_mix_pre, norm_mix_post, norm_ffn_pre, norm_ffn_post, lru_w_in, lru_b_in, lru_conv_w, lru_conv_b, lru_w_ga, lru_b_ga, lru_w_gx, lru_b_gx, lru_lambda, lru_w_out, lru_b_out, sb_w_qkv, sb_w_o, moba_w_qkv, moba_w_o, rel_bias, mlp_w_up, mlp_w_down):
    batch, seq, d = x.shape
    depth = norm_mix_pre.shape[0]
    n_mixers = 3
    xt = x.reshape(batch * seq, d)
    h = rmsnorm(xt, norm_mix_pre[0])
    for layer in range(depth):
        mixer = layer % n_mixers
        j = layer // n_mixers
        if mixer == 0:
            m = rglru_block(h, lru_w_in[j], lru_b_in[j], lru_conv_w[j], lru_conv_b[j],
                            lru_w_ga[j], lru_b_ga[j], lru_w_gx[j], lru_b_gx[j],
                            lru_lambda[j], lru_w_out[j], lru_b_out[j], batch, seq)
        elif mixer == 1:
            qkv = matmul(h, sb_w_qkv[j].astype(BF16), out_dtype=BF16)
            m = matmul(sb_attention(qkv, batch, seq), sb_w_o[j].astype(BF16))
        else:
            qkv = matmul(h, moba_w_qkv[j].astype(BF16), out_dtype=BF16)
            m = matmul(moba_attention(qkv, rel_bias, batch, seq), moba_w_o[j].astype(BF16))
        xt, h = resid_norm(xt, m, norm_mix_post[layer], norm_ffn_pre[layer])
        u = matmul(h, mlp_w_up[layer].astype(BF16), act="relu2", out_dtype=BF16)
        m = matmul(u, mlp_w_down[layer].astype(BF16))
        g_next = norm_mix_pre[layer + 1] if layer + 1 < depth else None
        xt, h = resid_norm(xt, m, norm_ffn_post[layer], g_next)
    return xt.reshape(batch, seq, d)
```

```python
import functools
import math

import jax
import jax.numpy as jnp
import numpy as np
from jax import lax
from jax.experimental import pallas as pl
from jax.experimental.pallas import tpu as pltpu

HEAD_DIM = 128
LANES = 128
MXU_WIDTH = 256
SUBLANES = 8
BF16_ROWS = 16
VMEM_LIMIT_BYTES = 56 * 1024 * 1024
NORM_EPS = 1e-6
LRU_C = 8.0
LRU_CHANNEL_TILE = 2688
MOBA_BLOCK = 256
MOBA_TOPK = 3
REL_MAX_DIST = 128
NEG_BIG = -1e30
MASK_BIG = 2.0 ** 100
PIPE = 4
SB_UNDERFLOW = -105.0

BF16 = jnp.bfloat16
F32 = jnp.float32


def _pick(dim, target, align):
    best = None
    for t in range(align, min(dim, target) + 1, align):
        if dim % t == 0:
            best = t
    assert best is not None, (dim, target, align)
    return best


def _params(*sem):
    return pltpu.CompilerParams(dimension_semantics=sem, vmem_limit_bytes=VMEM_LIMIT_BYTES)


def _rms_scale(v):
    return lax.rsqrt(jnp.mean(v * v, axis=-1, keepdims=True) + NORM_EPS)


def _rmsnorm_kernel(x_ref, g_ref, h_ref):
    x = x_ref[...]
    h_ref[...] = (x * _rms_scale(x) * g_ref[...]).astype(h_ref.dtype)


def rmsnorm(x, g):
    n, d = x.shape
    tr = _pick(n, 256, SUBLANES)
    return pl.pallas_call(
        _rmsnorm_kernel,
        out_shape=jax.ShapeDtypeStruct((n, d), BF16),
        grid=(n // tr,),
        in_specs=[pl.BlockSpec((tr, d), lambda i: (i, 0)),
                  pl.BlockSpec((1, d), lambda i: (0, 0))],
        out_specs=pl.BlockSpec((tr, d), lambda i: (i, 0)),
        compiler_params=_params("parallel"),
        name="rmsnorm",
    )(x, g.reshape(1, d))


def _resid_norm_kernel(x_ref, m_ref, gpost_ref, gnext_ref, xo_ref, ho_ref):
    m = m_ref[...]
    xn = x_ref[...] + m * _rms_scale(m) * gpost_ref[...]
    xo_ref[...] = xn
    ho_ref[...] = (xn * _rms_scale(xn) * gnext_ref[...]).astype(ho_ref.dtype)


def _resid_kernel(x_ref, m_ref, gpost_ref, xo_ref):
    m = m_ref[...]
    xo_ref[...] = x_ref[...] + m * _rms_scale(m) * gpost_ref[...]


def resid_norm(x, m, g_post, g_next):
    n, d = x.shape
    tr = _pick(n, 256, SUBLANES)
    row = pl.BlockSpec((tr, d), lambda i: (i, 0))
    vec = pl.BlockSpec((1, d), lambda i: (0, 0))
    if g_next is None:
        return pl.pallas_call(
            _resid_kernel,
            out_shape=jax.ShapeDtypeStruct((n, d), F32),
            grid=(n // tr,), in_specs=[row, row, vec], out_specs=row,
            compiler_params=_params("parallel"), name="resid",
        )(x, m, g_post.reshape(1, d)), None
    return pl.pallas_call(
        _resid_norm_kernel,
        out_shape=(jax.ShapeDtypeStruct((n, d), F32), jax.ShapeDtypeStruct((n, d), BF16)),
        grid=(n // tr,), in_specs=[row, row, vec, vec], out_specs=(row, row),
        compiler_params=_params("parallel"), name="resid_norm",
    )(x, m, g_post.reshape(1, d), g_next.reshape(1, d))


def _epilogue(r, bias_ref, act):
    if bias_ref is not None:
        r = r + bias_ref[...]
    if act == "relu2":
        r = jnp.maximum(r, 0.0)
        r = r * r
    return r


def _mm_kernel(*refs, nk, has_bias, act):
    a_ref, b_ref = refs[0], refs[1]
    bias_ref = refs[2] if has_bias else None
    o_ref = refs[2 + has_bias]
    part = jnp.dot(a_ref[...], b_ref[...], preferred_element_type=F32)
    if nk == 1:
        o_ref[...] = _epilogue(part, bias_ref, act).astype(o_ref.dtype)
        return
    acc_ref = refs[3 + has_bias]
    k = pl.program_id(2)

    @pl.when(k == 0)
    def _():
        acc_ref[...] = part

    @pl.when((k > 0) & (k < nk - 1))
    def _():
        acc_ref[...] += part

    @pl.when(k == nk - 1)
    def _():
        o_ref[...] = _epilogue(acc_ref[...] + part, bias_ref, act).astype(o_ref.dtype)


def matmul(a, b, bias=None, act=None, out_dtype=F32):
    b, layer = b
    m, kdim = a.shape
    _, _, n = b.shape
    tm = _pick(m, 1024, SUBLANES)
    tn = _pick(n, 1024, MXU_WIDTH if n % MXU_WIDTH == 0 else LANES)
    tk = _pick(kdim, 4096, LANES)
    nk = kdim // tk
    in_specs = [pl.BlockSpec((tm, tk), lambda i, j, k: (i, k)),
                pl.BlockSpec((None, tk, tn), lambda i, j, k: (layer, k, j))]
    args = [a, b]
    if bias is not None:
        in_specs.append(pl.BlockSpec((1, tn), lambda i, j, k: (0, j)))
        args.append(bias.reshape(1, n).astype(F32))
    return pl.pallas_call(
        functools.partial(_mm_kernel, nk=nk, has_bias=bias is not None, act=act),
        out_shape=jax.ShapeDtypeStruct((m, n), out_dtype),
        grid=(m // tm, n // tn, nk),
        in_specs=in_specs,
        out_specs=pl.BlockSpec((tm, tn), lambda i, j, k: (i, j)),
        scratch_shapes=[pltpu.VMEM((tm, tn), F32)] if nk > 1 else [],
        compiler_params=_params("parallel", "parallel", "arbitrary"),
        name="matmul",
    )(*args)


def _softplus(x):
    return jnp.maximum(x, 0.0) + jnp.log1p(jnp.exp(-jnp.abs(x)))


def _neg_expm1(x, exp_x):
    series = (((x * (1.0 / 24) + (1.0 / 6)) * x + 0.5) * x + 1.0) * (-x)
    return jnp.where(x > -1.0 / 32, series, 1.0 - exp_x)


def _lru_kernel(g_ref, x_ref, cw_ref, cb_ref, wa_ref, ba_ref, wx_ref, bx_ref, lam_ref,
                y_ref, ext_ref, a_ref, hs_ref, hcar_ref, *, ts, conv_width, windows):
    pad = SUBLANES

    @pl.when(pl.program_id(2) == 0)
    def _():
        ext_ref[0:pad, :] = jnp.zeros((pad, ext_ref.shape[1]), F32)
        hcar_ref[...] = jnp.zeros_like(hcar_ref)

    xcur = x_ref[...]
    ext_ref[pad:pad + ts, :] = xcur
    xb = cb_ref[...] + cw_ref[conv_width - 1:conv_width, :] * xcur
    for k in range(conv_width - 1):
        off = pad - (conv_width - 1) + k
        xb = xb + cw_ref[k:k + 1, :] * ext_ref[off:off + ts, :]
    ext_ref[0:pad, :] = xcur[ts - pad:ts, :]

    xb16 = xb.astype(BF16)

    def gate(w_ref, b_ref, z_ref):
        z_ref[...] = jnp.broadcast_to(b_ref[...], z_ref.shape)
        for m, (a0, wd) in enumerate(windows):
            z_ref[:, a0:a0 + wd] += jnp.dot(xb16[:, a0:a0 + wd], w_ref[m, 0:wd, 0:wd],
                                            preferred_element_type=F32)
        return jax.nn.sigmoid(z_ref[...])

    r = gate(wa_ref, ba_ref, a_ref)
    i = gate(wx_ref, bx_ref, hs_ref)
    log_a = (-LRU_C * r) * _softplus(-lam_ref[...])
    a = jnp.exp(log_a)
    a_ref[...] = a
    var = _neg_expm1(2.0 * log_a, a * a)
    hs_ref[...] = (var * lax.rsqrt(jnp.maximum(var, 1e-30))) * (i * xb)

    def step(t, h):
        h = a_ref[pl.ds(t, 1), :] * h + hs_ref[pl.ds(t, 1), :]
        hs_ref[pl.ds(t, 1), :] = h
        return h

    hcar_ref[0:1, :] = lax.fori_loop(0, ts, step, hcar_ref[0:1, :], unroll=8)
    y_ref[...] = (hs_ref[...] * jax.nn.gelu(g_ref[...], approximate=True)).astype(y_ref.dtype)


def _gate_windows(bw, nbs):
    starts = [bw * m // LANES * LANES for m in range(nbs)]
    ends = [-(-bw * (m + 1) // LANES) * LANES for m in range(nbs)]
    windows = tuple((a0, b0 - a0) for a0, b0 in zip(starts, ends))
    offsets = [bw * m - a0 for m, a0 in enumerate(starts)]
    return windows, offsets


def _window_pad(w, offsets, wmax):
    nblk, bw, _ = w.shape
    nbs = len(offsets)
    parts = [jnp.pad(w[m::nbs], ((0, 0), (o, wmax - o - bw), (o, wmax - o - bw)))
             for m, o in enumerate(offsets)]
    return jnp.stack(parts, axis=1).reshape(nblk, wmax, wmax).astype(BF16)


def lru_core(u, conv_w, conv_b, w_ga, b_ga, w_gx, b_gx, lam, batch, seq):
    n, two_r = u.shape
    r = two_r // 2
    nblk, bw, _ = w_ga.shape
    ts = _pick(seq, 256, SUBLANES)
    nts = seq // ts
    cw = _pick(r, LRU_CHANNEL_TILE, bw * LANES // math.gcd(bw, LANES))
    nbs, ncw = cw // bw, r // cw
    windows, offsets = _gate_windows(bw, nbs)
    wmax = max(wd for _, wd in windows)
    conv_width = conv_w.shape[0]
    assert conv_width - 1 <= SUBLANES and nblk * bw == r
    tok = lambda b, c, s: (b * nts + s, c)
    vec = pl.BlockSpec((1, cw), lambda b, c, s: (0, c))
    wblk = pl.BlockSpec((nbs, wmax, wmax), lambda b, c, s: (c, 0, 0))
    row = lambda v: v.reshape(1, r)
    return pl.pallas_call(
        functools.partial(_lru_kernel, ts=ts, conv_width=conv_width, windows=windows),
        out_shape=jax.ShapeDtypeStruct((n, r), BF16),
        grid=(batch, ncw, nts),
        in_specs=[pl.BlockSpec((ts, cw), tok),
                  pl.BlockSpec((ts, cw), lambda b, c, s: (b * nts + s, ncw + c)),
                  pl.BlockSpec((conv_width, cw), lambda b, c, s: (0, c)),
                  vec, wblk, vec, wblk, vec, vec],
        out_specs=pl.BlockSpec((ts, cw), tok),
        scratch_shapes=[pltpu.VMEM((ts + SUBLANES, cw), F32),
                        pltpu.VMEM((ts, cw), F32),
                        pltpu.VMEM((ts, cw), F32),
                        pltpu.VMEM((SUBLANES, cw), F32)],
        compiler_params=_params("parallel", "parallel", "arbitrary"),
        name="lru_core",
    )(u, u, conv_w, row(conv_b), _window_pad(w_ga, offsets, wmax), row(b_ga),
      _window_pad(w_gx, offsets, wmax), row(b_gx), row(lam))


def rglru_block(h, w_in, b_in, conv_w, conv_b, w_ga, b_ga, w_gx, b_gx, lam, w_out, b_out,
                batch, seq):
    u = matmul(h, w_in, bias=b_in)
    y = lru_core(u, conv_w, conv_b, w_ga, b_ga, w_gx, b_gx, lam, batch, seq)
    return matmul(y, w_out, bias=b_out)


def _qk(q, k):
    return lax.dot_general(q, k, (((1,), (1,)), ((), ())), preferred_element_type=F32)


def _pipeline_iters(n_steps, lead):
    return -(-(n_steps + lead) // PIPE) * PIPE


def _step_table(rows, lead):
    n = len(rows)
    idx = np.clip(np.arange(_pipeline_iters(n, lead) + lead) - lead, 0, n - 1)
    return jnp.asarray(np.asarray(rows, np.int32)[idx].T)


def _sb_kernel(tab_ref, q_ref, k_ref, v_ref, o_ref, z_ref, e_ref, r_ref, acc_ref,
               upper_ref, keep_ref, pen_ref, r_all_ref, acc_all_ref, *, tq, n_steps, nq):
    scale = HEAD_DIM ** -0.5
    row = lax.broadcasted_iota(jnp.int32, (tq, tq), 0)
    col = lax.broadcasted_iota(jnp.int32, (tq, tq), 1)
    upper = (row > col).astype(BF16)
    upper_ref[0:tq, :] = upper
    upper_ref[tq:2 * tq, :] = upper
    strict = col < row
    keep_ref[0:tq, :] = jnp.ones((tq, tq), F32)
    keep_ref[tq:2 * tq, :] = jnp.where(strict, 1.0, 0.0)
    pen_ref[0:tq, :] = jnp.zeros((tq, tq), F32)
    pen_ref[tq:2 * tq, :] = jnp.where(strict, 0.0, NEG_BIG)
    z_ref[...] = jnp.zeros_like(z_ref)
    e_ref[...] = jnp.zeros_like(e_ref)
    r_ref[...] = jnp.zeros_like(r_ref)
    acc_ref[...] = jnp.zeros_like(acc_ref)

    def rows(t):
        return pl.ds(pl.multiple_of(t * tq, tq), tq)

    def iteration(it, par):
        qc, kc, first_c = tab_ref[0, it], tab_ref[1, it], tab_ref[2, it]
        w = jnp.exp(e_ref[par]).astype(BF16)
        acc = jnp.where(first_c == 1, 0.0, acc_ref[...]) + jnp.dot(
            w, v_ref[rows(kc), :], preferred_element_type=F32)
        acc_ref[...] = acc
        acc_all_ref[rows(qc), :] = acc
        o_ref[rows(qc), :] = acc.astype(o_ref.dtype)
        qb, first_b = tab_ref[0, it + PIPE], tab_ref[2, it + PIPE]
        diag = pl.ds(pl.multiple_of(first_b * tq, tq), tq)
        z = z_ref[par]
        sp = jnp.log(1.0 + jnp.exp(-jnp.abs(z)))
        log_beta = jnp.minimum(z, 0.0) - sp
        log_keep = (log_beta - z) * keep_ref[diag, :]
        hi = log_keep.astype(BF16)
        lo = (log_keep - hi.astype(F32)).astype(BF16)
        later = jnp.dot(jnp.concatenate([hi, lo], axis=1), upper_ref[...],
                        preferred_element_type=F32)
        r_in = jnp.where(first_b == 1, 0.0, r_ref[...])
        e_ref[par] = log_beta + later + r_in + pen_ref[diag, :]
        r_out = r_in + jnp.sum(log_keep, axis=1, keepdims=True)
        r_ref[...] = r_out
        r_all_ref[rows(qb), :] = r_out
        qa, ka = tab_ref[0, it + 2 * PIPE], tab_ref[1, it + 2 * PIPE]
        z_ref[par] = _qk(q_ref[rows(qa), :], k_ref[rows(ka), :]) * scale

    def body(j, carry):
        for par in range(PIPE):
            iteration(PIPE * j + par, par)
        return carry

    lax.fori_loop(0, _pipeline_iters(n_steps, 2 * PIPE) // PIPE, body, 0)

    def alive(r):
        return jnp.max(r) > SB_UNDERFLOW

    def older_tiles(i, carry):
        r0 = r_all_ref[rows(i), :]

        @pl.when(alive(r0))
        def _():
            q = q_ref[rows(i), :]

            def more(c):
                return (c[0] >= 0) & alive(c[1])

            def tile(c):
                kb, r_in, acc = c
                z = _qk(q, k_ref[rows(kb), :]) * scale
                sp = jnp.log(1.0 + jnp.exp(-jnp.abs(z)))
                log_beta = jnp.minimum(z, 0.0) - sp
                log_keep = log_beta - z
                hi = log_keep.astype(BF16)
                lo = (log_keep - hi.astype(F32)).astype(BF16)
                later = jnp.dot(jnp.concatenate([hi, lo], axis=1), upper_ref[...],
                                preferred_element_type=F32)
                w = jnp.exp(log_beta + later + r_in).astype(BF16)
                acc = acc + jnp.dot(w, v_ref[rows(kb), :], preferred_element_type=F32)
                return kb - 1, r_in + jnp.sum(log_keep, axis=1, keepdims=True), acc

            c = lax.while_loop(more, tile, (i - 2, r0, acc_all_ref[rows(i), :]))
            o_ref[rows(i), :] = c[2].astype(o_ref.dtype)

        return carry

    lax.fori_loop(2, nq, older_tiles, 0)


def _attn_specs(n_heads, seq):
    q_spec = pl.BlockSpec((seq, HEAD_DIM), lambda b, h: (b, h))
    k_spec = pl.BlockSpec((seq, HEAD_DIM), lambda b, h: (b, n_heads + h))
    v_spec = pl.BlockSpec((seq, HEAD_DIM), lambda b, h: (b, 2 * n_heads + h))
    return q_spec, k_spec, v_spec


def sb_attention(qkv, batch, seq):
    n, three_d = qkv.shape
    n_heads = three_d // (3 * HEAD_DIM)
    tq = _pick(seq, 256, LANES)
    nq = seq // tq
    steps = [(i, kb, int(kb == i)) for i in range(nq) for kb in range(i, max(i - 2, -1), -1)]
    steps.append(steps[0])
    q_spec, k_spec, v_spec = _attn_specs(n_heads, seq)
    return pl.pallas_call(
        functools.partial(_sb_kernel, tq=tq, n_steps=len(steps), nq=nq),
        out_shape=jax.ShapeDtypeStruct((n, n_heads * HEAD_DIM), BF16),
        grid=(batch, n_heads),
        in_specs=[pl.BlockSpec(memory_space=pltpu.SMEM), q_spec, k_spec, v_spec],
        out_specs=q_spec,
        scratch_shapes=[pltpu.VMEM((PIPE, tq, tq), F32),
                        pltpu.VMEM((PIPE, tq, tq), F32),
                        pltpu.VMEM((tq, 1), F32),
                        pltpu.VMEM((tq, HEAD_DIM), F32),
                        pltpu.VMEM((2 * tq, tq), BF16),
                        pltpu.VMEM((2 * tq, tq), F32),
                        pltpu.VMEM((2 * tq, tq), F32),
                        pltpu.VMEM((seq, 1), F32),
                        pltpu.VMEM((seq, HEAD_DIM), F32)],
        compiler_params=_params("parallel", "parallel"),
        name="sb_attention",
    )(_step_table(steps, 2 * PIPE), qkv, qkv, qkv)


def _t5_bucket(dist, n_buckets):
    max_exact = n_buckets // 2
    n_f = jnp.maximum(dist, 1).astype(F32)
    large = max_exact + (jnp.log(n_f / max_exact) / math.log(REL_MAX_DIST / max_exact)
                         * (n_buckets - max_exact)).astype(jnp.int32)
    large = jnp.minimum(large, n_buckets - 1)
    return jnp.where(dist < max_exact, dist, large)


def _moba_kernel(tab_ref, bias_ref, q_ref, k_ref, v_ref, o_ref,
                 sel_ref, pos_ref, s_ref, m_ref, acc_ref, *, seq, n_buckets, nkp, n_steps):
    blk = MOBA_BLOCK
    h = pl.program_id(1)
    nkb = seq // blk
    nkr = -(-nkb // BF16_ROWS) * BF16_ROWS
    scale = HEAD_DIM ** -0.5

    r = lax.broadcasted_iota(jnp.int32, (nkr, seq), 0)
    c = lax.broadcasted_iota(jnp.int32, (nkr, seq), 1)
    lo_edge = r * blk
    avg = jnp.where((c >= lo_edge) & (c < lo_edge + blk), 1.0 / blk, 0.0).astype(BF16)
    km = jnp.dot(avg, k_ref[...], preferred_element_type=F32)
    km_hi = km.astype(BF16)
    km_lo = (km - km_hi.astype(F32)).astype(BF16)

    q_all = q_ref[...]
    gate = _qk(km_hi, q_all) + _qk(km_lo, q_all)
    neg_inf = jnp.float32(-jnp.inf)
    g = jnp.where(lo_edge + blk <= c, gate, neg_inf)
    r_f = r.astype(F32)
    sel = jnp.zeros((nkr, seq), F32)
    for _ in range(max(1, min(MOBA_TOPK, nkb - 1))):
        mx = jnp.max(g, axis=0, keepdims=True)
        first = jnp.min(jnp.where(g == mx, r_f, float(nkr)), axis=0, keepdims=True)
        pick = (r_f == first) & (mx > neg_inf)
        sel = jnp.where(pick, 1.0, sel)
        g = jnp.where(pick, neg_inf, g)
    sel_t = (sel - 1.0).astype(BF16)
    row = lax.broadcasted_iota(jnp.int32, (blk, blk), 0)
    col = lax.broadcasted_iota(jnp.int32, (blk, blk), 1)
    eye = (row == col).astype(BF16)
    pad = jnp.zeros((nkp - nkr, blk), BF16)
    for i in range(nkb):
        cols = jnp.concatenate([sel_t[:, i * blk:(i + 1) * blk], pad], axis=0)
        sel_ref[i * blk:(i + 1) * blk, :] = _qk(eye, cols).astype(BF16)

    mcol = lax.broadcasted_iota(jnp.int32, (SUBLANES, 2 * blk), 1)
    dist = jnp.where(mcol <= blk, blk - mcol, 3 * blk - mcol)
    bucket = _t5_bucket(dist, n_buckets)
    vals = jnp.zeros((SUBLANES, 2 * blk), F32)
    for bk in range(n_buckets):
        vals = jnp.where(bucket == bk, bias_ref[bk, h], vals)
    tile = pltpu.roll(jnp.broadcast_to(vals[0:1, :], (blk, 2 * blk)), 0, 1, stride=1, stride_axis=0)
    pos_ref[0:blk, :] = tile[:, 0:blk]
    pos_ref[blk:2 * blk, :] = jnp.where(col <= row, tile[:, blk:2 * blk], NEG_BIG)
    pos_ref[2 * blk:3 * blk, :] = jnp.full((blk, blk), bias_ref[n_buckets - 1, h], F32)

    def rows(t):
        return pl.ds(pl.multiple_of(t * blk, blk), blk)

    s_ref[...] = jnp.zeros_like(s_ref)
    m_ref[...] = jnp.zeros_like(m_ref)
    acc_ref[...] = jnp.zeros_like(acc_ref)
    ones = jnp.ones((blk, HEAD_DIM), BF16)
    key_lane = lax.broadcasted_iota(jnp.int32, (blk, nkp), 1)

    def iteration(it, par):
        qb, kb, own_b = tab_ref[0, it], tab_ref[1, it], tab_ref[2, it]
        s = s_ref[par]
        m_old = jnp.where(own_b == 1, NEG_BIG, m_ref[...])
        m_new = jnp.maximum(m_old, jnp.broadcast_to(jnp.max(s, axis=1, keepdims=True),
                                                    (blk, HEAD_DIM)))
        p = jnp.exp(s - jnp.concatenate([m_new, m_new], axis=1))
        alpha = jnp.exp(m_old - m_new)
        v_ext = jnp.concatenate([v_ref[rows(kb), :], ones], axis=1)
        acc = (jnp.concatenate([alpha, alpha], axis=1) * acc_ref[...]
               + jnp.dot(p.astype(BF16), v_ext, preferred_element_type=F32))
        m_ref[...] = m_new
        acc_ref[...] = acc
        o_ref[rows(qb), :] = (acc[:, 0:HEAD_DIM] / acc[:, HEAD_DIM:2 * HEAD_DIM]).astype(o_ref.dtype)
        qa, ka, pos_a, lane_a = (tab_ref[0, it + PIPE], tab_ref[1, it + PIPE],
                                 tab_ref[3, it + PIPE], tab_ref[4, it + PIPE])
        q_ext = jnp.concatenate([q_ref[rows(qa), :], sel_ref[rows(qa), :]], axis=1)
        k_ext = jnp.concatenate(
            [k_ref[rows(ka), :], jnp.where(key_lane == lane_a, MASK_BIG, 0.0).astype(BF16)], axis=1)
        pos = pos_ref[pl.ds(pl.multiple_of(pos_a, blk), blk), :]
        s_ref[par] = _qk(q_ext, k_ext) * scale + pos

    def body(j, carry):
        for par in range(PIPE):
            iteration(PIPE * j + par, par)
        return carry

    lax.fori_loop(0, _pipeline_iters(n_steps, PIPE) // PIPE, body, 0)


def moba_attention(qkv, rel_bias, batch, seq):
    n, three_d = qkv.shape
    n_heads = three_d // (3 * HEAD_DIM)
    blk = MOBA_BLOCK
    assert seq % blk == 0 and REL_MAX_DIST <= blk
    nq = seq // blk
    nkp = -(-nq // LANES) * LANES
    n_buckets = rel_bias.shape[0]
    assert nq < nkp
    steps = []
    for i in range(nq):
        steps.append((i, i, 1, blk, nkp - 1))
        if i >= 1:
            steps.append((i, i - 1, 0, 0, i - 1))
        steps.extend((i, kb, 0, 2 * blk, kb) for kb in range(i - 1))
    steps.append(steps[0])
    smem = pl.BlockSpec(memory_space=pltpu.SMEM)
    q_spec, k_spec, v_spec = _attn_specs(n_heads, seq)
    return pl.pallas_call(
        functools.partial(_moba_kernel, seq=seq, n_buckets=n_buckets, nkp=nkp, n_steps=len(steps)),
        out_shape=jax.ShapeDtypeStruct((n, n_heads * HEAD_DIM), BF16),
        grid=(batch, n_heads),
        in_specs=[smem, smem, q_spec, k_spec, v_spec],
        out_specs=q_spec,
        scratch_shapes=[pltpu.VMEM((seq, nkp), BF16),
                        pltpu.VMEM((3 * blk, blk), F32),
                        pltpu.VMEM((PIPE, blk, blk), F32),
                        pltpu.VMEM((blk, HEAD_DIM), F32),
                        pltpu.VMEM((blk, 2 * HEAD_DIM), F32)],
        compiler_params=_params("parallel", "parallel"),
        name="moba_attention",
    )(_step_table(steps, PIPE), rel_bias.astype(F32), qkv, qkv, qkv)


def kernel(x, norm_mix_pre, norm_mix_post, norm_ffn_pre, norm_ffn_post, lru_w_in, lru_b_in, lru_conv_w, lru_conv_b, lru_w_ga, lru_b_ga, lru_w_gx, lru_b_gx, lru_lambda, lru_w_out, lru_b_out, sb_w_qkv, sb_w_o, moba_w_qkv, moba_w_o, rel_bias, mlp_w_up, mlp_w_down):
    batch, seq, d = x.shape
    depth = norm_mix_pre.shape[0]
    n_mixers = 3
    xt = x.reshape(batch * seq, d)
    lru_w_in, lru_w_out, sb_w_qkv, sb_w_o, moba_w_qkv, moba_w_o, mlp_w_up, mlp_w_down = (
        w.astype(BF16) for w in (lru_w_in, lru_w_out, sb_w_qkv, sb_w_o, moba_w_qkv, moba_w_o,
                                 mlp_w_up, mlp_w_down))
    h = rmsnorm(xt, norm_mix_pre[0])
    for layer in range(depth):
        mixer = layer % n_mixers
        j = layer // n_mixers
        if mixer == 0:
            m = rglru_block(h, (lru_w_in, j), lru_b_in[j], lru_conv_w[j], lru_conv_b[j],
                            lru_w_ga[j], lru_b_ga[j], lru_w_gx[j], lru_b_gx[j],
                            lru_lambda[j], (lru_w_out, j), lru_b_out[j], batch, seq)
        elif mixer == 1:
            qkv = matmul(h, (sb_w_qkv, j), out_dtype=BF16)
            m = matmul(sb_attention(qkv, batch, seq), (sb_w_o, j))
        else:
            qkv = matmul(h, (moba_w_qkv, j), out_dtype=BF16)
            m = matmul(moba_attention(qkv, rel_bias, batch, seq), (moba_w_o, j))
        xt, h = resid_norm(xt, m, norm_mix_post[layer], norm_ffn_pre[layer])
        u = matmul(h, (mlp_w_up, layer), act="relu2", out_dtype=BF16)
        m = matmul(u, (mlp_w_down, layer))
        g_next = norm_mix_pre[layer + 1] if layer + 1 < depth else None
        xt, h = resid_norm(xt, m, norm_ffn_post[layer], g_next)
    return xt.reshape(batch, seq, d)
```

```python
import functools
import math

import jax
import jax.numpy as jnp
import numpy as np
from jax import lax
from jax.experimental import pallas as pl
from jax.experimental.pallas import tpu as pltpu

HEAD_DIM = 128
LANES = 128
MXU_WIDTH = 256
SUBLANES = 8
BF16_ROWS = 16
VMEM_LIMIT_BYTES = 56 * 1024 * 1024
NORM_EPS = 1e-6
LRU_C = 8.0
LRU_CHANNEL_TILE = 2688
MOBA_BLOCK = 256
MOBA_TOPK = 3
REL_MAX_DIST = 128
NEG_BIG = -1e30
MASK_BIG = 2.0 ** 100
SB_PIPE = 2
MOBA_PIPE = 4
SB_UNDERFLOW = -105.0
LOG2E = math.log2(math.e)

BF16 = jnp.bfloat16
F32 = jnp.float32


def _pick(dim, target, align):
    best = None
    for t in range(align, min(dim, target) + 1, align):
        if dim % t == 0:
            best = t
    assert best is not None, (dim, target, align)
    return best


def _params(*sem):
    return pltpu.CompilerParams(dimension_semantics=sem, vmem_limit_bytes=VMEM_LIMIT_BYTES)


def _rms_scale(v):
    return lax.rsqrt(jnp.mean(v * v, axis=-1, keepdims=True) + NORM_EPS)


def _rmsnorm_kernel(x_ref, g_ref, h_ref):
    x = x_ref[...]
    h_ref[...] = (x * _rms_scale(x) * g_ref[...]).astype(h_ref.dtype)


def rmsnorm(x, g):
    n, d = x.shape
    tr = _pick(n, 256, SUBLANES)
    return pl.pallas_call(
        _rmsnorm_kernel,
        out_shape=jax.ShapeDtypeStruct((n, d), BF16),
        grid=(n // tr,),
        in_specs=[pl.BlockSpec((tr, d), lambda i: (i, 0)),
                  pl.BlockSpec((1, d), lambda i: (0, 0))],
        out_specs=pl.BlockSpec((tr, d), lambda i: (i, 0)),
        compiler_params=_params("parallel"),
        name="rmsnorm",
    )(x, g.reshape(1, d))


def _resid_norm_kernel(x_ref, m_ref, gpost_ref, gnext_ref, xo_ref, ho_ref):
    m = m_ref[...]
    xn = x_ref[...] + m * _rms_scale(m) * gpost_ref[...]
    xo_ref[...] = xn
    ho_ref[...] = (xn * _rms_scale(xn) * gnext_ref[...]).astype(ho_ref.dtype)


def _resid_kernel(x_ref, m_ref, gpost_ref, xo_ref):
    m = m_ref[...]
    xo_ref[...] = x_ref[...] + m * _rms_scale(m) * gpost_ref[...]


def resid_norm(x, m, g_post, g_next):
    n, d = x.shape
    tr = _pick(n, 256, SUBLANES)
    row = pl.BlockSpec((tr, d), lambda i: (i, 0))
    vec = pl.BlockSpec((1, d), lambda i: (0, 0))
    if g_next is None:
        return pl.pallas_call(
            _resid_kernel,
            out_shape=jax.ShapeDtypeStruct((n, d), F32),
            grid=(n // tr,), in_specs=[row, row, vec], out_specs=row,
            compiler_params=_params("parallel"), name="resid",
        )(x, m, g_post.reshape(1, d)), None
    return pl.pallas_call(
        _resid_norm_kernel,
        out_shape=(jax.ShapeDtypeStruct((n, d), F32), jax.ShapeDtypeStruct((n, d), BF16)),
        grid=(n // tr,), in_specs=[row, row, vec, vec], out_specs=(row, row),
        compiler_params=_params("parallel"), name="resid_norm",
    )(x, m, g_post.reshape(1, d), g_next.reshape(1, d))


def _epilogue(r, bias_ref, act):
    if bias_ref is not None:
        r = r + bias_ref[...]
    if act == "relu2":
        r = jnp.maximum(r, 0.0)
        r = r * r
    return r


def _mm_kernel(*refs, nk, has_bias, act):
    a_ref, b_ref = refs[0], refs[1]
    bias_ref = refs[2] if has_bias else None
    o_ref = refs[2 + has_bias]
    part = jnp.dot(a_ref[...], b_ref[...], preferred_element_type=F32)
    if nk == 1:
        o_ref[...] = _epilogue(part, bias_ref, act).astype(o_ref.dtype)
        return
    acc_ref = refs[3 + has_bias]
    k = pl.program_id(2)

    @pl.when(k == 0)
    def _():
        acc_ref[...] = part

    @pl.when((k > 0) & (k < nk - 1))
    def _():
        acc_ref[...] += part

    @pl.when(k == nk - 1)
    def _():
        o_ref[...] = _epilogue(acc_ref[...] + part, bias_ref, act).astype(o_ref.dtype)


def matmul(a, b, bias=None, act=None, out_dtype=F32):
    b, layer = b
    m, kdim = a.shape
    _, _, n = b.shape
    tm = _pick(m, 1024, SUBLANES)
    tn = _pick(n, 1024, MXU_WIDTH if n % MXU_WIDTH == 0 else LANES)
    tk = _pick(kdim, 4096, LANES)
    nk = kdim // tk
    in_specs = [pl.BlockSpec((tm, tk), lambda i, j, k: (i, k)),
                pl.BlockSpec((None, tk, tn), lambda i, j, k: (layer, k, j))]
    args = [a, b]
    if bias is not None:
        in_specs.append(pl.BlockSpec((1, tn), lambda i, j, k: (0, j)))
        args.append(bias.reshape(1, n).astype(F32))
    return pl.pallas_call(
        functools.partial(_mm_kernel, nk=nk, has_bias=bias is not None, act=act),
        out_shape=jax.ShapeDtypeStruct((m, n), out_dtype),
        grid=(m // tm, n // tn, nk),
        in_specs=in_specs,
        out_specs=pl.BlockSpec((tm, tn), lambda i, j, k: (i, j)),
        scratch_shapes=[pltpu.VMEM((tm, tn), F32)] if nk > 1 else [],
        compiler_params=_params("parallel", "parallel", "arbitrary"),
        name="matmul",
    )(*args)


def _softplus(x):
    return jnp.maximum(x, 0.0) + jnp.log1p(jnp.exp(-jnp.abs(x)))


def _neg_expm1(x, exp_x):
    series = (((x * (1.0 / 24) + (1.0 / 6)) * x + 0.5) * x + 1.0) * (-x)
    return jnp.where(x > -1.0 / 32, series, 1.0 - exp_x)


def _lru_kernel(g_ref, x_ref, cw_ref, cb_ref, wa_ref, ba_ref, wx_ref, bx_ref, lam_ref,
                y_ref, ext_ref, a_ref, hs_ref, hcar_ref, *, ts, conv_width, windows):
    pad = SUBLANES

    @pl.when(pl.program_id(2) == 0)
    def _():
        ext_ref[0:pad, :] = jnp.zeros((pad, ext_ref.shape[1]), F32)
        hcar_ref[...] = jnp.zeros_like(hcar_ref)

    xcur = x_ref[...]
    ext_ref[pad:pad + ts, :] = xcur
    xb = cb_ref[...] + cw_ref[conv_width - 1:conv_width, :] * xcur
    for k in range(conv_width - 1):
        off = pad - (conv_width - 1) + k
        xb = xb + cw_ref[k:k + 1, :] * ext_ref[off:off + ts, :]
    ext_ref[0:pad, :] = xcur[ts - pad:ts, :]

    xb16 = xb.astype(BF16)

    def gate(w_ref, b_ref, z_ref):
        z_ref[...] = jnp.broadcast_to(b_ref[...], z_ref.shape)
        for m, (a0, wd) in enumerate(windows):
            z_ref[:, a0:a0 + wd] += jnp.dot(xb16[:, a0:a0 + wd], w_ref[m, 0:wd, 0:wd],
                                            preferred_element_type=F32)
        return jax.nn.sigmoid(z_ref[...])

    r = gate(wa_ref, ba_ref, a_ref)
    i = gate(wx_ref, bx_ref, hs_ref)
    log_a = (-LRU_C * r) * _softplus(-lam_ref[...])
    a = jnp.exp(log_a)
    a_ref[...] = a
    var = _neg_expm1(2.0 * log_a, a * a)
    hs_ref[...] = (var * lax.rsqrt(jnp.maximum(var, 1e-30))) * (i * xb)

    def step(t, h):
        h = a_ref[pl.ds(t, 1), :] * h + hs_ref[pl.ds(t, 1), :]
        hs_ref[pl.ds(t, 1), :] = h
        return h

    hcar_ref[0:1, :] = lax.fori_loop(0, ts, step, hcar_ref[0:1, :], unroll=8)
    y_ref[...] = (hs_ref[...] * jax.nn.gelu(g_ref[...], approximate=True)).astype(y_ref.dtype)


def _gate_windows(bw, nbs):
    starts = [bw * m // LANES * LANES for m in range(nbs)]
    ends = [-(-bw * (m + 1) // LANES) * LANES for m in range(nbs)]
    windows = tuple((a0, b0 - a0) for a0, b0 in zip(starts, ends))
    offsets = [bw * m - a0 for m, a0 in enumerate(starts)]
    return windows, offsets


def _window_pad(w, offsets, wmax):
    nblk, bw, _ = w.shape
    nbs = len(offsets)
    parts = [jnp.pad(w[m::nbs], ((0, 0), (o, wmax - o - bw), (o, wmax - o - bw)))
             for m, o in enumerate(offsets)]
    return jnp.stack(parts, axis=1).reshape(nblk, wmax, wmax).astype(BF16)


def lru_core(u, conv_w, conv_b, w_ga, b_ga, w_gx, b_gx, lam, batch, seq):
    n, two_r = u.shape
    r = two_r // 2
    nblk, bw, _ = w_ga.shape
    ts = _pick(seq, 256, SUBLANES)
    nts = seq // ts
    cw = _pick(r, LRU_CHANNEL_TILE, bw * LANES // math.gcd(bw, LANES))
    nbs, ncw = cw // bw, r // cw
    windows, offsets = _gate_windows(bw, nbs)
    wmax = max(wd for _, wd in windows)
    conv_width = conv_w.shape[0]
    assert conv_width - 1 <= SUBLANES and nblk * bw == r
    tok = lambda b, c, s: (b * nts + s, c)
    vec = pl.BlockSpec((1, cw), lambda b, c, s: (0, c))
    wblk = pl.BlockSpec((nbs, wmax, wmax), lambda b, c, s: (c, 0, 0))
    row = lambda v: v.reshape(1, r)
    return pl.pallas_call(
        functools.partial(_lru_kernel, ts=ts, conv_width=conv_width, windows=windows),
        out_shape=jax.ShapeDtypeStruct((n, r), BF16),
        grid=(batch, ncw, nts),
        in_specs=[pl.BlockSpec((ts, cw), tok),
                  pl.BlockSpec((ts, cw), lambda b, c, s: (b * nts + s, ncw + c)),
                  pl.BlockSpec((conv_width, cw), lambda b, c, s: (0, c)),
                  vec, wblk, vec, wblk, vec, vec],
        out_specs=pl.BlockSpec((ts, cw), tok),
        scratch_shapes=[pltpu.VMEM((ts + SUBLANES, cw), F32),
                        pltpu.VMEM((ts, cw), F32),
                        pltpu.VMEM((ts, cw), F32),
                        pltpu.VMEM((SUBLANES, cw), F32)],
        compiler_params=_params("parallel", "parallel", "arbitrary"),
        name="lru_core",
    )(u, u, conv_w, row(conv_b), _window_pad(w_ga, offsets, wmax), row(b_ga),
      _window_pad(w_gx, offsets, wmax), row(b_gx), row(lam))


def rglru_block(h, w_in, b_in, conv_w, conv_b, w_ga, b_ga, w_gx, b_gx, lam, w_out, b_out,
                batch, seq):
    u = matmul(h, w_in, bias=b_in)
    y = lru_core(u, conv_w, conv_b, w_ga, b_ga, w_gx, b_gx, lam, batch, seq)
    return matmul(y, w_out, bias=b_out)


def _qk(q, k):
    return lax.dot_general(q, k, (((1,), (1,)), ((), ())), preferred_element_type=F32)


def _pipeline_iters(n_steps, lead, pipe):
    return -(-(n_steps + lead) // pipe) * pipe


def _step_table(rows, lead, pipe):
    n = len(rows)
    idx = np.clip(np.arange(_pipeline_iters(n, lead, pipe) + lead) - lead, 0, n - 1)
    return jnp.asarray(np.asarray(rows, np.int32)[idx].T)


def _sb_kernel(tab_ref, q_ref, k_ref, v_ref, o_ref, z_ref, e_ref, r_ref, acc_ref,
               upper_ref, keep_ref, pen_ref, r_all_ref, acc_all_ref, *, tq, n_steps, nq):
    scale = HEAD_DIM ** -0.5
    row = lax.broadcasted_iota(jnp.int32, (tq, tq), 0)
    col = lax.broadcasted_iota(jnp.int32, (tq, tq), 1)
    upper = (row > col).astype(BF16)
    upper_ref[0:tq, :] = upper
    upper_ref[tq:2 * tq, :] = upper
    strict = col < row
    keep_ref[0:tq, :] = jnp.ones((tq, tq), F32)
    keep_ref[tq:2 * tq, :] = jnp.where(strict, 1.0, 0.0)
    pen_ref[0:tq, :] = jnp.zeros((tq, tq), F32)
    pen_ref[tq:2 * tq, :] = jnp.where(strict, 0.0, NEG_BIG)
    z_ref[...] = jnp.zeros_like(z_ref)
    e_ref[...] = jnp.zeros_like(e_ref)
    r_ref[...] = jnp.zeros_like(r_ref)
    acc_ref[...] = jnp.zeros_like(acc_ref)

    def rows(t):
        return pl.ds(pl.multiple_of(t * tq, tq), tq)

    PIPE = SB_PIPE

    def iteration(it, par):
        qc, kc, first_c = tab_ref[0, it], tab_ref[1, it], tab_ref[2, it]
        w = jnp.exp(e_ref[par]).astype(BF16)
        acc = jnp.where(first_c == 1, 0.0, acc_ref[...]) + jnp.dot(
            w, v_ref[rows(kc), :], preferred_element_type=F32)
        acc_ref[...] = acc
        acc_all_ref[rows(qc), :] = acc
        o_ref[rows(qc), :] = acc.astype(o_ref.dtype)
        qb, first_b = tab_ref[0, it + PIPE], tab_ref[2, it + PIPE]
        diag = pl.ds(pl.multiple_of(first_b * tq, tq), tq)
        z = z_ref[par]
        sp = jnp.log(1.0 + jnp.exp(-jnp.abs(z)))
        log_beta = jnp.minimum(z, 0.0) - sp
        log_keep = (log_beta - z) * keep_ref[diag, :]
        hi = log_keep.astype(BF16)
        lo = (log_keep - hi.astype(F32)).astype(BF16)
        later = jnp.dot(jnp.concatenate([hi, lo], axis=1), upper_ref[...],
                        preferred_element_type=F32)
        r_in = jnp.where(first_b == 1, 0.0, r_ref[...])
        e_ref[par] = log_beta + later + r_in + pen_ref[diag, :]
        r_out = r_in + jnp.sum(log_keep, axis=1, keepdims=True)
        r_ref[...] = r_out
        r_all_ref[rows(qb), :] = r_out
        qa, ka = tab_ref[0, it + 2 * PIPE], tab_ref[1, it + 2 * PIPE]
        z_ref[par] = _qk(q_ref[rows(qa), :], k_ref[rows(ka), :]) * scale

    def body(j, carry):
        for par in range(PIPE):
            iteration(PIPE * j + par, par)
        return carry

    lax.fori_loop(0, _pipeline_iters(n_steps, 2 * PIPE, PIPE) // PIPE, body, 0)

    def alive(r):
        return jnp.max(r) > SB_UNDERFLOW

    def older_tiles(i, carry):
        r0 = r_all_ref[rows(i), :]

        @pl.when(alive(r0))
        def _():
            q = q_ref[rows(i), :]

            def more(c):
                return (c[0] >= 0) & alive(c[1])

            def tile(c):
                kb, r_in, acc = c
                z = _qk(q, k_ref[rows(kb), :]) * scale
                sp = jnp.log(1.0 + jnp.exp(-jnp.abs(z)))
                log_beta = jnp.minimum(z, 0.0) - sp
                log_keep = log_beta - z
                hi = log_keep.astype(BF16)
                lo = (log_keep - hi.astype(F32)).astype(BF16)
                later = jnp.dot(jnp.concatenate([hi, lo], axis=1), upper_ref[...],
                                preferred_element_type=F32)
                w = jnp.exp(log_beta + later + r_in).astype(BF16)
                acc = acc + jnp.dot(w, v_ref[rows(kb), :], preferred_element_type=F32)
                return kb - 1, r_in + jnp.sum(log_keep, axis=1, keepdims=True), acc

            c = lax.while_loop(more, tile, (i - 2, r0, acc_all_ref[rows(i), :]))
            o_ref[rows(i), :] = c[2].astype(o_ref.dtype)

        return carry

    if nq > 2:
        @pl.when(alive(r_all_ref[2 * tq:nq * tq, :]))
        def _():
            lax.fori_loop(2, nq, older_tiles, 0)


def _attn_specs(n_heads, seq):
    q_spec = pl.BlockSpec((seq, HEAD_DIM), lambda b, h: (b, h))
    k_spec = pl.BlockSpec((seq, HEAD_DIM), lambda b, h: (b, n_heads + h))
    v_spec = pl.BlockSpec((seq, HEAD_DIM), lambda b, h: (b, 2 * n_heads + h))
    return q_spec, k_spec, v_spec


def sb_attention(qkv, batch, seq):
    n, three_d = qkv.shape
    n_heads = three_d // (3 * HEAD_DIM)
    tq = _pick(seq, 256, LANES)
    nq = seq // tq
    steps = [(i, kb, int(kb == i)) for i in range(nq) for kb in range(i, max(i - 2, -1), -1)]
    steps.append(steps[0])
    q_spec, k_spec, v_spec = _attn_specs(n_heads, seq)
    return pl.pallas_call(
        functools.partial(_sb_kernel, tq=tq, n_steps=len(steps), nq=nq),
        out_shape=jax.ShapeDtypeStruct((n, n_heads * HEAD_DIM), BF16),
        grid=(batch, n_heads),
        in_specs=[pl.BlockSpec(memory_space=pltpu.SMEM), q_spec, k_spec, v_spec],
        out_specs=q_spec,
        scratch_shapes=[pltpu.VMEM((SB_PIPE, tq, tq), F32),
                        pltpu.VMEM((SB_PIPE, tq, tq), F32),
                        pltpu.VMEM((tq, 1), F32),
                        pltpu.VMEM((tq, HEAD_DIM), F32),
                        pltpu.VMEM((2 * tq, tq), BF16),
                        pltpu.VMEM((2 * tq, tq), F32),
                        pltpu.VMEM((2 * tq, tq), F32),
                        pltpu.VMEM((seq, 1), F32),
                        pltpu.VMEM((seq, HEAD_DIM), F32)],
        compiler_params=_params("parallel", "parallel"),
        name="sb_attention",
    )(_step_table(steps, 2 * SB_PIPE, SB_PIPE), qkv, qkv, qkv)


def _t5_bucket(dist, n_buckets):
    max_exact = n_buckets // 2
    n_f = jnp.maximum(dist, 1).astype(F32)
    large = max_exact + (jnp.log(n_f / max_exact) / math.log(REL_MAX_DIST / max_exact)
                         * (n_buckets - max_exact)).astype(jnp.int32)
    large = jnp.minimum(large, n_buckets - 1)
    return jnp.where(dist < max_exact, dist, large)


def _moba_kernel(tab_ref, bias_ref, q_ref, k_ref, v_ref, o_ref,
                 sel_ref, pos_ref, s_ref, m_ref, acc_ref, *, seq, n_buckets, nkp, n_steps):
    blk = MOBA_BLOCK
    h = pl.program_id(1)
    nkb = seq // blk
    nkr = -(-nkb // BF16_ROWS) * BF16_ROWS
    scale = HEAD_DIM ** -0.5

    r = lax.broadcasted_iota(jnp.int32, (nkr, seq), 0)
    c = lax.broadcasted_iota(jnp.int32, (nkr, seq), 1)
    lo_edge = r * blk
    avg = jnp.where((c >= lo_edge) & (c < lo_edge + blk), 1.0 / blk, 0.0).astype(BF16)
    km = jnp.dot(avg, k_ref[...], preferred_element_type=F32)
    km_hi = km.astype(BF16)
    km_lo = (km - km_hi.astype(F32)).astype(BF16)

    q_all = q_ref[...]
    gate = _qk(km_hi, q_all) + _qk(km_lo, q_all)
    neg_inf = jnp.float32(-jnp.inf)
    g = jnp.where(lo_edge + blk <= c, gate, neg_inf)
    r_f = r.astype(F32)
    sel = jnp.zeros((nkr, seq), F32)
    for _ in range(max(1, min(MOBA_TOPK, nkb - 1))):
        mx = jnp.max(g, axis=0, keepdims=True)
        first = jnp.min(jnp.where(g == mx, r_f, float(nkr)), axis=0, keepdims=True)
        pick = (r_f == first) & (mx > neg_inf)
        sel = jnp.where(pick, 1.0, sel)
        g = jnp.where(pick, neg_inf, g)
    sel_t = (sel - 1.0).astype(BF16)
    row = lax.broadcasted_iota(jnp.int32, (blk, blk), 0)
    col = lax.broadcasted_iota(jnp.int32, (blk, blk), 1)
    eye = (row == col).astype(BF16)
    pad = jnp.zeros((nkp - nkr, blk), BF16)
    for i in range(nkb):
        cols = jnp.concatenate([sel_t[:, i * blk:(i + 1) * blk], pad], axis=0)
        sel_ref[i * blk:(i + 1) * blk, :] = _qk(eye, cols).astype(BF16)

    mcol = lax.broadcasted_iota(jnp.int32, (SUBLANES, 2 * blk), 1)
    dist = jnp.where(mcol <= blk, blk - mcol, 3 * blk - mcol)
    bucket = _t5_bucket(dist, n_buckets)
    vals = jnp.zeros((SUBLANES, 2 * blk), F32)
    for bk in range(n_buckets):
        vals = jnp.where(bucket == bk, bias_ref[bk, h], vals)
    tile = pltpu.roll(jnp.broadcast_to(vals[0:1, :], (blk, 2 * blk)), 0, 1, stride=1, stride_axis=0)
    tile = tile * LOG2E
    pos_ref[0:blk, :] = tile[:, 0:blk]
    pos_ref[blk:2 * blk, :] = jnp.where(col <= row, tile[:, blk:2 * blk], NEG_BIG)
    pos_ref[2 * blk:3 * blk, :] = jnp.full((blk, blk), bias_ref[n_buckets - 1, h] * LOG2E, F32)

    def rows(t):
        return pl.ds(pl.multiple_of(t * blk, blk), blk)

    s_ref[...] = jnp.zeros_like(s_ref)
    m_ref[...] = jnp.zeros_like(m_ref)
    acc_ref[...] = jnp.zeros_like(acc_ref)
    ones = jnp.ones((blk, HEAD_DIM), BF16)
    key_lane = lax.broadcasted_iota(jnp.int32, (blk, nkp), 1)

    PIPE = MOBA_PIPE

    def iteration(it, par):
        qb, kb, own_b = tab_ref[0, it], tab_ref[1, it], tab_ref[2, it]
        s = s_ref[par]
        m_old = jnp.where(own_b == 1, NEG_BIG, m_ref[...])
        m_new = jnp.maximum(m_old, jnp.broadcast_to(jnp.max(s, axis=1, keepdims=True),
                                                    (blk, HEAD_DIM)))
        p = jnp.exp2(s - jnp.concatenate([m_new, m_new], axis=1))
        alpha = jnp.exp2(m_old - m_new)
        v_ext = jnp.concatenate([v_ref[rows(kb), :], ones], axis=1)
        acc = (jnp.concatenate([alpha, alpha], axis=1) * acc_ref[...]
               + jnp.dot(p.astype(BF16), v_ext, preferred_element_type=F32))
        m_ref[...] = m_new
        acc_ref[...] = acc
        o_ref[rows(qb), :] = (acc[:, 0:HEAD_DIM] / acc[:, HEAD_DIM:2 * HEAD_DIM]).astype(o_ref.dtype)
        qa, ka, pos_a, lane_a = (tab_ref[0, it + PIPE], tab_ref[1, it + PIPE],
                                 tab_ref[3, it + PIPE], tab_ref[4, it + PIPE])
        q_ext = jnp.concatenate([q_ref[rows(qa), :], sel_ref[rows(qa), :]], axis=1)
        k_ext = jnp.concatenate(
            [k_ref[rows(ka), :], jnp.where(key_lane == lane_a, MASK_BIG, 0.0).astype(BF16)], axis=1)
        pos = pos_ref[pl.ds(pl.multiple_of(pos_a, blk), blk), :]
        s_ref[par] = _qk(q_ext, k_ext) * (scale * LOG2E) + pos

    def body(j, carry):
        for par in range(PIPE):
            iteration(PIPE * j + par, par)
        return carry

    lax.fori_loop(0, _pipeline_iters(n_steps, PIPE, PIPE) // PIPE, body, 0)


def moba_attention(qkv, rel_bias, batch, seq):
    n, three_d = qkv.shape
    n_heads = three_d // (3 * HEAD_DIM)
    blk = MOBA_BLOCK
    assert seq % blk == 0 and REL_MAX_DIST <= blk
    nq = seq // blk
    nkp = -(-nq // LANES) * LANES
    n_buckets = rel_bias.shape[0]
    assert nq < nkp
    steps = []
    for i in range(nq):
        steps.append((i, i, 1, blk, nkp - 1))
        if i >= 1:
            steps.append((i, i - 1, 0, 0, i - 1))
        steps.extend((i, kb, 0, 2 * blk, kb) for kb in range(i - 1))
    steps.append(steps[0])
    smem = pl.BlockSpec(memory_space=pltpu.SMEM)
    q_spec, k_spec, v_spec = _attn_specs(n_heads, seq)
    return pl.pallas_call(
        functools.partial(_moba_kernel, seq=seq, n_buckets=n_buckets, nkp=nkp, n_steps=len(steps)),
        out_shape=jax.ShapeDtypeStruct((n, n_heads * HEAD_DIM), BF16),
        grid=(batch, n_heads),
        in_specs=[smem, smem, q_spec, k_spec, v_spec],
        out_specs=q_spec,
        scratch_shapes=[pltpu.VMEM((seq, nkp), BF16),
                        pltpu.VMEM((3 * blk, blk), F32),
                        pltpu.VMEM((MOBA_PIPE, blk, blk), F32),
                        pltpu.VMEM((blk, HEAD_DIM), F32),
                        pltpu.VMEM((blk, 2 * HEAD_DIM), F32)],
        compiler_params=_params("parallel", "parallel"),
        name="moba_attention",
    )(_step_table(steps, MOBA_PIPE, MOBA_PIPE), rel_bias.astype(F32), qkv, qkv, qkv)


def kernel(x, norm_mix_pre, norm_mix_post, norm_ffn_pre, norm_ffn_post, lru_w_in, lru_b_in, lru_conv_w, lru_conv_b, lru_w_ga, lru_b_ga, lru_w_gx, lru_b_gx, lru_lambda, lru_w_out, lru_b_out, sb_w_qkv, sb_w_o, moba_w_qkv, moba_w_o, rel_bias, mlp_w_up, mlp_w_down):
    batch, seq, d = x.shape
    depth = norm_mix_pre.shape[0]
    n_mixers = 3
    xt = x.reshape(batch * seq, d)
    lru_w_in, lru_w_out, sb_w_qkv, sb_w_o, moba_w_qkv, moba_w_o, mlp_w_up, mlp_w_down = (
        w.astype(BF16) for w in (lru_w_in, lru_w_out, sb_w_qkv, sb_w_o, moba_w_qkv, moba_w_o,
                                 mlp_w_up, mlp_w_down))
    h = rmsnorm(xt, norm_mix_pre[0])
    for layer in range(depth):
        mixer = layer % n_mixers
        j = layer // n_mixers
        if mixer == 0:
            m = rglru_block(h, (lru_w_in, j), lru_b_in[j], lru_conv_w[j], lru_conv_b[j],
                            lru_w_ga[j], lru_b_ga[j], lru_w_gx[j], lru_b_gx[j],
                            lru_lambda[j], (lru_w_out, j), lru_b_out[j], batch, seq)
        elif mixer == 1:
            qkv = matmul(h, (sb_w_qkv, j), out_dtype=BF16)
            m = matmul(sb_attention(qkv, batch, seq), (sb_w_o, j))
        else:
            qkv = matmul(h, (moba_w_qkv, j), out_dtype=BF16)
            m = matmul(moba_attention(qkv, rel_bias, batch, seq), (moba_w_o, j))
        xt, h = resid_norm(xt, m, norm_mix_post[layer], norm_ffn_pre[layer])
        u = matmul(h, (mlp_w_up, layer), act="relu2", out_dtype=BF16)
        m = matmul(u, (mlp_w_down, layer))
        g_next = norm_mix_pre[layer + 1] if layer + 1 < depth else None
        xt, h = resid_norm(xt, m, norm_ffn_post[layer], g_next)
    return xt.reshape(batch, seq, d)
```

```python
import functools
import math

import jax
import jax.numpy as jnp
import numpy as np
from jax import lax
from jax.experimental import pallas as pl
from jax.experimental.pallas import tpu as pltpu

HEAD_DIM = 128
LANES = 128
MXU_WIDTH = 256
SUBLANES = 8
BF16_ROWS = 16
VMEM_LIMIT_BYTES = 56 * 1024 * 1024
NORM_EPS = 1e-6
LRU_C = 8.0
LRU_CHANNEL_TILE = 2688
MOBA_BLOCK = 256
MOBA_TOPK = 3
REL_MAX_DIST = 128
NEG_BIG = -1e30
MASK_BIG = 2.0 ** 100
SB_PIPE = 2
MOBA_PIPE = 4
SB_UNDERFLOW = -105.0
LOG2E = math.log2(math.e)

BF16 = jnp.bfloat16
F32 = jnp.float32


def _pick(dim, target, align):
    best = None
    for t in range(align, min(dim, target) + 1, align):
        if dim % t == 0:
            best = t
    assert best is not None, (dim, target, align)
    return best


def _params(*sem):
    return pltpu.CompilerParams(dimension_semantics=sem, vmem_limit_bytes=VMEM_LIMIT_BYTES)


def _rms_scale(v):
    return lax.rsqrt(jnp.mean(v * v, axis=-1, keepdims=True) + NORM_EPS)


def _rmsnorm_kernel(x_ref, g_ref, h_ref):
    x = x_ref[...]
    h_ref[...] = (x * _rms_scale(x) * g_ref[...]).astype(h_ref.dtype)


def rmsnorm(x, g):
    n, d = x.shape
    tr = _pick(n, 256, SUBLANES)
    return pl.pallas_call(
        _rmsnorm_kernel,
        out_shape=jax.ShapeDtypeStruct((n, d), BF16),
        grid=(n // tr,),
        in_specs=[pl.BlockSpec((tr, d), lambda i: (i, 0)),
                  pl.BlockSpec((1, d), lambda i: (0, 0))],
        out_specs=pl.BlockSpec((tr, d), lambda i: (i, 0)),
        compiler_params=_params("parallel"),
        name="rmsnorm",
    )(x, g.reshape(1, d))


def _resid_norm_kernel(x_ref, m_ref, gpost_ref, gnext_ref, xo_ref, ho_ref):
    m = m_ref[...]
    xn = x_ref[...] + m * _rms_scale(m) * gpost_ref[...]
    xo_ref[...] = xn
    ho_ref[...] = (xn * _rms_scale(xn) * gnext_ref[...]).astype(ho_ref.dtype)


def _resid_kernel(x_ref, m_ref, gpost_ref, xo_ref):
    m = m_ref[...]
    xo_ref[...] = x_ref[...] + m * _rms_scale(m) * gpost_ref[...]


def resid_norm(x, m, g_post, g_next):
    n, d = x.shape
    tr = _pick(n, 256, SUBLANES)
    row = pl.BlockSpec((tr, d), lambda i: (i, 0))
    vec = pl.BlockSpec((1, d), lambda i: (0, 0))
    if g_next is None:
        return pl.pallas_call(
            _resid_kernel,
            out_shape=jax.ShapeDtypeStruct((n, d), F32),
            grid=(n // tr,), in_specs=[row, row, vec], out_specs=row,
            compiler_params=_params("parallel"), name="resid",
        )(x, m, g_post.reshape(1, d)), None
    return pl.pallas_call(
        _resid_norm_kernel,
        out_shape=(jax.ShapeDtypeStruct((n, d), F32), jax.ShapeDtypeStruct((n, d), BF16)),
        grid=(n // tr,), in_specs=[row, row, vec, vec], out_specs=(row, row),
        compiler_params=_params("parallel"), name="resid_norm",
    )(x, m, g_post.reshape(1, d), g_next.reshape(1, d))


def _epilogue(r, bias_ref, act):
    if bias_ref is not None:
        r = r + bias_ref[...]
    if act == "relu2":
        r = jnp.maximum(r, 0.0)
        r = r * r
    return r


def _mm_kernel(*refs, nk, has_bias, act):
    a_ref, b_ref = refs[0], refs[1]
    bias_ref = refs[2] if has_bias else None
    o_ref = refs[2 + has_bias]
    part = jnp.dot(a_ref[...], b_ref[...], preferred_element_type=F32)
    if nk == 1:
        o_ref[...] = _epilogue(part, bias_ref, act).astype(o_ref.dtype)
        return
    acc_ref = refs[3 + has_bias]
    k = pl.program_id(2)

    @pl.when(k == 0)
    def _():
        acc_ref[...] = part

    @pl.when((k > 0) & (k < nk - 1))
    def _():
        acc_ref[...] += part

    @pl.when(k == nk - 1)
    def _():
        o_ref[...] = _epilogue(acc_ref[...] + part, bias_ref, act).astype(o_ref.dtype)


def matmul(a, b, bias=None, act=None, out_dtype=F32):
    b, layer = b
    m, kdim = a.shape
    _, _, n = b.shape
    tm = _pick(m, 1024, SUBLANES)
    tn = _pick(n, 1024, MXU_WIDTH if n % MXU_WIDTH == 0 else LANES)
    tk = _pick(kdim, 4096, LANES)
    nk = kdim // tk
    in_specs = [pl.BlockSpec((tm, tk), lambda i, j, k: (i, k)),
                pl.BlockSpec((None, tk, tn), lambda i, j, k: (layer, k, j))]
    args = [a, b]
    if bias is not None:
        in_specs.append(pl.BlockSpec((1, tn), lambda i, j, k: (0, j)))
        args.append(bias.reshape(1, n).astype(F32))
    return pl.pallas_call(
        functools.partial(_mm_kernel, nk=nk, has_bias=bias is not None, act=act),
        out_shape=jax.ShapeDtypeStruct((m, n), out_dtype),
        grid=(m // tm, n // tn, nk),
        in_specs=in_specs,
        out_specs=pl.BlockSpec((tm, tn), lambda i, j, k: (i, j)),
        scratch_shapes=[pltpu.VMEM((tm, tn), F32)] if nk > 1 else [],
        compiler_params=_params("parallel", "parallel", "arbitrary"),
        name="matmul",
    )(*args)


def _softplus(x):
    return jnp.maximum(x, 0.0) + jnp.log1p(jnp.exp(-jnp.abs(x)))


def _neg_expm1(x, exp_x):
    series = (((x * (1.0 / 24) + (1.0 / 6)) * x + 0.5) * x + 1.0) * (-x)
    return jnp.where(x > -1.0 / 32, series, 1.0 - exp_x)


def _lru_kernel(g_ref, x_ref, cw_ref, cb_ref, wa_ref, ba_ref, wx_ref, bx_ref, lam_ref,
                y_ref, ext_ref, a_ref, hs_ref, hcar_ref, *, ts, conv_width, windows):
    pad = SUBLANES

    @pl.when(pl.program_id(2) == 0)
    def _():
        ext_ref[0:pad, :] = jnp.zeros((pad, ext_ref.shape[1]), F32)
        hcar_ref[...] = jnp.zeros_like(hcar_ref)

    xcur = x_ref[...]
    ext_ref[pad:pad + ts, :] = xcur
    xb = cb_ref[...] + cw_ref[conv_width - 1:conv_width, :] * xcur
    for k in range(conv_width - 1):
        off = pad - (conv_width - 1) + k
        xb = xb + cw_ref[k:k + 1, :] * ext_ref[off:off + ts, :]
    ext_ref[0:pad, :] = xcur[ts - pad:ts, :]

    xb16 = xb.astype(BF16)

    def gate(w_ref, b_ref, z_ref):
        z_ref[...] = jnp.broadcast_to(b_ref[...], z_ref.shape)
        for m, (a0, wd) in enumerate(windows):
            z_ref[:, a0:a0 + wd] += jnp.dot(xb16[:, a0:a0 + wd], w_ref[m, 0:wd, 0:wd],
                                            preferred_element_type=F32)
        return jax.nn.sigmoid(z_ref[...])

    r = gate(wa_ref, ba_ref, a_ref)
    i = gate(wx_ref, bx_ref, hs_ref)
    log_a = (-LRU_C * r) * _softplus(-lam_ref[...])
    a = jnp.exp(log_a)
    a_ref[...] = a
    var = _neg_expm1(2.0 * log_a, a * a)
    hs_ref[...] = (var * lax.rsqrt(jnp.maximum(var, 1e-30))) * (i * xb)

    def step(t, h):
        h = a_ref[pl.ds(t, 1), :] * h + hs_ref[pl.ds(t, 1), :]
        hs_ref[pl.ds(t, 1), :] = h
        return h

    hcar_ref[0:1, :] = lax.fori_loop(0, ts, step, hcar_ref[0:1, :], unroll=8)
    y_ref[...] = (hs_ref[...] * jax.nn.gelu(g_ref[...], approximate=True)).astype(y_ref.dtype)


def _gate_windows(bw, nbs):
    starts = [bw * m // LANES * LANES for m in range(nbs)]
    ends = [-(-bw * (m + 1) // LANES) * LANES for m in range(nbs)]
    windows = tuple((a0, b0 - a0) for a0, b0 in zip(starts, ends))
    offsets = [bw * m - a0 for m, a0 in enumerate(starts)]
    return windows, offsets


def _window_pad(w, offsets, wmax):
    nblk, bw, _ = w.shape
    nbs = len(offsets)
    parts = [jnp.pad(w[m::nbs], ((0, 0), (o, wmax - o - bw), (o, wmax - o - bw)))
             for m, o in enumerate(offsets)]
    return jnp.stack(parts, axis=1).reshape(nblk, wmax, wmax).astype(BF16)


def lru_core(u, conv_w, conv_b, w_ga, b_ga, w_gx, b_gx, lam, batch, seq):
    n, two_r = u.shape
    r = two_r // 2
    nblk, bw, _ = w_ga.shape
    ts = _pick(seq, 256, SUBLANES)
    nts = seq // ts
    cw = _pick(r, LRU_CHANNEL_TILE, bw * LANES // math.gcd(bw, LANES))
    nbs, ncw = cw // bw, r // cw
    windows, offsets = _gate_windows(bw, nbs)
    wmax = max(wd for _, wd in windows)
    conv_width = conv_w.shape[0]
    assert conv_width - 1 <= SUBLANES and nblk * bw == r
    tok = lambda b, c, s: (b * nts + s, c)
    vec = pl.BlockSpec((1, cw), lambda b, c, s: (0, c))
    wblk = pl.BlockSpec((nbs, wmax, wmax), lambda b, c, s: (c, 0, 0))
    row = lambda v: v.reshape(1, r)
    return pl.pallas_call(
        functools.partial(_lru_kernel, ts=ts, conv_width=conv_width, windows=windows),
        out_shape=jax.ShapeDtypeStruct((n, r), BF16),
        grid=(batch, ncw, nts),
        in_specs=[pl.BlockSpec((ts, cw), tok),
                  pl.BlockSpec((ts, cw), lambda b, c, s: (b * nts + s, ncw + c)),
                  pl.BlockSpec((conv_width, cw), lambda b, c, s: (0, c)),
                  vec, wblk, vec, wblk, vec, vec],
        out_specs=pl.BlockSpec((ts, cw), tok),
        scratch_shapes=[pltpu.VMEM((ts + SUBLANES, cw), F32),
                        pltpu.VMEM((ts, cw), F32),
                        pltpu.VMEM((ts, cw), F32),
                        pltpu.VMEM((SUBLANES, cw), F32)],
        compiler_params=_params("parallel", "parallel", "arbitrary"),
        name="lru_core",
    )(u, u, conv_w, row(conv_b), _window_pad(w_ga, offsets, wmax), row(b_ga),
      _window_pad(w_gx, offsets, wmax), row(b_gx), row(lam))


def rglru_block(h, w_in, b_in, conv_w, conv_b, w_ga, b_ga, w_gx, b_gx, lam, w_out, b_out,
                batch, seq):
    u = matmul(h, w_in, bias=b_in)
    y = lru_core(u, conv_w, conv_b, w_ga, b_ga, w_gx, b_gx, lam, batch, seq)
    return matmul(y, w_out, bias=b_out)


def _qk(q, k):
    return lax.dot_general(q, k, (((1,), (1,)), ((), ())), preferred_element_type=F32)


def _pipeline_iters(n_steps, lead, pipe):
    return -(-(n_steps + lead) // pipe) * pipe


def _step_table(rows, lead, pipe):
    n = len(rows)
    idx = np.clip(np.arange(_pipeline_iters(n, lead, pipe) + lead) - lead, 0, n - 1)
    return jnp.asarray(np.asarray(rows, np.int32)[idx].T)


def _sb_kernel(tab_ref, q_ref, k_ref, v_ref, o_ref, z_ref, e_ref, r_ref, acc_ref,
               upper_ref, keep_ref, pen_ref, r_all_ref, acc_all_ref, *, tq, n_steps, nq):
    scale = HEAD_DIM ** -0.5
    row = lax.broadcasted_iota(jnp.int32, (tq, tq), 0)
    col = lax.broadcasted_iota(jnp.int32, (tq, tq), 1)
    upper = (row > col).astype(BF16)
    upper_ref[0:tq, :] = upper
    upper_ref[tq:2 * tq, :] = upper
    strict = col < row
    keep_ref[0:tq, :] = jnp.ones((tq, tq), F32)
    keep_ref[tq:2 * tq, :] = jnp.where(strict, 1.0, 0.0)
    pen_ref[0:tq, :] = jnp.zeros((tq, tq), F32)
    pen_ref[tq:2 * tq, :] = jnp.where(strict, 0.0, NEG_BIG)
    z_ref[...] = jnp.zeros_like(z_ref)
    e_ref[...] = jnp.zeros_like(e_ref)
    r_ref[...] = jnp.zeros_like(r_ref)
    acc_ref[...] = jnp.zeros_like(acc_ref)

    def rows(t):
        return pl.ds(pl.multiple_of(t * tq, tq), tq)

    PIPE = SB_PIPE

    def iteration(it, par):
        qc, kc, first_c = tab_ref[0, it], tab_ref[1, it], tab_ref[2, it]
        w = jnp.exp(e_ref[par]).astype(BF16)
        acc = jnp.where(first_c == 1, 0.0, acc_ref[...]) + jnp.dot(
            w, v_ref[rows(kc), :], preferred_element_type=F32)
        acc_ref[...] = acc
        acc_all_ref[rows(qc), :] = acc
        o_ref[rows(qc), :] = acc.astype(o_ref.dtype)
        qb, first_b = tab_ref[0, it + PIPE], tab_ref[2, it + PIPE]
        diag = pl.ds(pl.multiple_of(first_b * tq, tq), tq)
        z = z_ref[par]
        sp = jnp.log(1.0 + jnp.exp(-jnp.abs(z)))
        log_beta = jnp.minimum(z, 0.0) - sp
        log_keep = (log_beta - z) * keep_ref[diag, :]
        hi = log_keep.astype(BF16)
        lo = (log_keep - hi.astype(F32)).astype(BF16)
        later = jnp.dot(jnp.concatenate([hi, lo], axis=1), upper_ref[...],
                        preferred_element_type=F32)
        r_in = jnp.where(first_b == 1, 0.0, r_ref[...])
        e_ref[par] = log_beta + later + r_in + pen_ref[diag, :]
        r_out = r_in + jnp.sum(log_keep, axis=1, keepdims=True)
        r_ref[...] = r_out
        r_all_ref[rows(qb), :] = r_out
        qa, ka = tab_ref[0, it + 2 * PIPE], tab_ref[1, it + 2 * PIPE]
        z_ref[par] = _qk(q_ref[rows(qa), :], k_ref[rows(ka), :]) * scale

    def body(j, carry):
        for par in range(PIPE):
            iteration(PIPE * j + par, par)
        return carry

    lax.fori_loop(0, _pipeline_iters(n_steps, 2 * PIPE, PIPE) // PIPE, body, 0)

    def alive(r):
        return jnp.max(r) > SB_UNDERFLOW

    def older_tiles(i, carry):
        r0 = r_all_ref[rows(i), :]

        @pl.when(alive(r0))
        def _():
            q = q_ref[rows(i), :]

            def more(c):
                return (c[0] >= 0) & alive(c[1])

            def tile(c):
                kb, r_in, acc = c
                z = _qk(q, k_ref[rows(kb), :]) * scale
                sp = jnp.log(1.0 + jnp.exp(-jnp.abs(z)))
                log_beta = jnp.minimum(z, 0.0) - sp
                log_keep = log_beta - z
                hi = log_keep.astype(BF16)
                lo = (log_keep - hi.astype(F32)).astype(BF16)
                later = jnp.dot(jnp.concatenate([hi, lo], axis=1), upper_ref[...],
                                preferred_element_type=F32)
                w = jnp.exp(log_beta + later + r_in).astype(BF16)
                acc = acc + jnp.dot(w, v_ref[rows(kb), :], preferred_element_type=F32)
                return kb - 1, r_in + jnp.sum(log_keep, axis=1, keepdims=True), acc

            c = lax.while_loop(more, tile, (i - 2, r0, acc_all_ref[rows(i), :]))
            o_ref[rows(i), :] = c[2].astype(o_ref.dtype)

        return carry

    if nq > 2:
        @pl.when(alive(r_all_ref[2 * tq:nq * tq, :]))
        def _():
            lax.fori_loop(2, nq, older_tiles, 0)


def _attn_specs(n_heads, seq):
    q_spec = pl.BlockSpec((seq, HEAD_DIM), lambda b, h: (b, h))
    k_spec = pl.BlockSpec((seq, HEAD_DIM), lambda b, h: (b, n_heads + h))
    v_spec = pl.BlockSpec((seq, HEAD_DIM), lambda b, h: (b, 2 * n_heads + h))
    return q_spec, k_spec, v_spec


def sb_attention(qkv, batch, seq):
    n, three_d = qkv.shape
    n_heads = three_d // (3 * HEAD_DIM)
    tq = _pick(seq, 256, LANES)
    nq = seq // tq
    steps = [(i, kb, int(kb == i)) for i in range(nq) for kb in range(i, max(i - 2, -1), -1)]
    steps.append(steps[0])
    q_spec, k_spec, v_spec = _attn_specs(n_heads, seq)
    return pl.pallas_call(
        functools.partial(_sb_kernel, tq=tq, n_steps=len(steps), nq=nq),
        out_shape=jax.ShapeDtypeStruct((n, n_heads * HEAD_DIM), BF16),
        grid=(batch, n_heads),
        in_specs=[pl.BlockSpec(memory_space=pltpu.SMEM), q_spec, k_spec, v_spec],
        out_specs=q_spec,
        scratch_shapes=[pltpu.VMEM((SB_PIPE, tq, tq), F32),
                        pltpu.VMEM((SB_PIPE, tq, tq), F32),
                        pltpu.VMEM((tq, 1), F32),
                        pltpu.VMEM((tq, HEAD_DIM), F32),
                        pltpu.VMEM((2 * tq, tq), BF16),
                        pltpu.VMEM((2 * tq, tq), F32),
                        pltpu.VMEM((2 * tq, tq), F32),
                        pltpu.VMEM((seq, 1), F32),
                        pltpu.VMEM((seq, HEAD_DIM), F32)],
        compiler_params=_params("parallel", "parallel"),
        name="sb_attention",
    )(_step_table(steps, 2 * SB_PIPE, SB_PIPE), qkv, qkv, qkv)


def _t5_bucket(dist, n_buckets):
    max_exact = n_buckets // 2
    n_f = jnp.maximum(dist, 1).astype(F32)
    large = max_exact + (jnp.log(n_f / max_exact) / math.log(REL_MAX_DIST / max_exact)
                         * (n_buckets - max_exact)).astype(jnp.int32)
    large = jnp.minimum(large, n_buckets - 1)
    return jnp.where(dist < max_exact, dist, large)


def _moba_kernel(tab_ref, bias_ref, q_ref, k_ref, v_ref, o_ref,
                 sel_ref, pos_ref, s_ref, m_ref, acc_ref, acc_all_ref,
                 *, seq, n_buckets, nkp, n_steps):
    blk = MOBA_BLOCK
    h = pl.program_id(1)
    nkb = seq // blk
    nkr = -(-nkb // BF16_ROWS) * BF16_ROWS
    scale = HEAD_DIM ** -0.5

    r = lax.broadcasted_iota(jnp.int32, (nkr, seq), 0)
    c = lax.broadcasted_iota(jnp.int32, (nkr, seq), 1)
    lo_edge = r * blk
    avg = jnp.where((c >= lo_edge) & (c < lo_edge + blk), 1.0 / blk, 0.0).astype(BF16)
    km = jnp.dot(avg, k_ref[...], preferred_element_type=F32)
    km_hi = km.astype(BF16)
    km_lo = (km - km_hi.astype(F32)).astype(BF16)

    q_all = q_ref[...]
    gate = _qk(km_hi, q_all) + _qk(km_lo, q_all)
    neg_inf = jnp.float32(-jnp.inf)
    g = jnp.where(lo_edge + blk <= c, gate, neg_inf)
    r_f = r.astype(F32)
    sel = jnp.zeros((nkr, seq), F32)
    for _ in range(max(1, min(MOBA_TOPK, nkb - 1))):
        mx = jnp.max(g, axis=0, keepdims=True)
        first = jnp.min(jnp.where(g == mx, r_f, float(nkr)), axis=0, keepdims=True)
        pick = (r_f == first) & (mx > neg_inf)
        sel = jnp.where(pick, 1.0, sel)
        g = jnp.where(pick, neg_inf, g)
    sel_t = (sel - 1.0).astype(BF16)
    row = lax.broadcasted_iota(jnp.int32, (blk, blk), 0)
    col = lax.broadcasted_iota(jnp.int32, (blk, blk), 1)
    eye = (row == col).astype(BF16)
    pad = jnp.zeros((nkp - nkr, blk), BF16)
    for i in range(nkb):
        cols = jnp.concatenate([sel_t[:, i * blk:(i + 1) * blk], pad], axis=0)
        sel_ref[i * blk:(i + 1) * blk, :] = _qk(eye, cols).astype(BF16)

    mcol = lax.broadcasted_iota(jnp.int32, (SUBLANES, 2 * blk), 1)
    dist = jnp.where(mcol <= blk, blk - mcol, 3 * blk - mcol)
    bucket = _t5_bucket(dist, n_buckets)
    vals = jnp.zeros((SUBLANES, 2 * blk), F32)
    for bk in range(n_buckets):
        vals = jnp.where(bucket == bk, bias_ref[bk, h], vals)
    tile = pltpu.roll(jnp.broadcast_to(vals[0:1, :], (blk, 2 * blk)), 0, 1, stride=1, stride_axis=0)
    tile = tile * LOG2E
    pos_ref[0:blk, :] = tile[:, 0:blk]
    pos_ref[blk:2 * blk, :] = jnp.where(col <= row, tile[:, blk:2 * blk], NEG_BIG)
    pos_ref[2 * blk:3 * blk, :] = jnp.full((blk, blk), bias_ref[n_buckets - 1, h] * LOG2E, F32)

    def rows(t):
        return pl.ds(pl.multiple_of(t * blk, blk), blk)

    s_ref[...] = jnp.zeros_like(s_ref)
    m_ref[...] = jnp.zeros_like(m_ref)
    acc_ref[...] = jnp.zeros_like(acc_ref)
    ones = jnp.ones((blk, HEAD_DIM), BF16)
    key_lane = lax.broadcasted_iota(jnp.int32, (blk, nkp), 1)

    PIPE = MOBA_PIPE

    def iteration(it, par):
        qb, kb, own_b = tab_ref[0, it], tab_ref[1, it], tab_ref[2, it]
        s = s_ref[par]
        m_old = jnp.where(own_b == 1, NEG_BIG, m_ref[...])
        m_new = jnp.maximum(m_old, jnp.broadcast_to(jnp.max(s, axis=1, keepdims=True),
                                                    (blk, HEAD_DIM)))
        p = jnp.exp2(s - jnp.concatenate([m_new, m_new], axis=1))
        alpha = jnp.exp2(m_old - m_new)
        v_ext = jnp.concatenate([v_ref[rows(kb), :], ones], axis=1)
        acc = (jnp.concatenate([alpha, alpha], axis=1) * acc_ref[...]
               + jnp.dot(p.astype(BF16), v_ext, preferred_element_type=F32))
        m_ref[...] = m_new
        acc_ref[...] = acc
        acc_all_ref[rows(qb), :] = acc
        qa, ka, pos_a, lane_a = (tab_ref[0, it + PIPE], tab_ref[1, it + PIPE],
                                 tab_ref[3, it + PIPE], tab_ref[4, it + PIPE])
        q_ext = jnp.concatenate([q_ref[rows(qa), :], sel_ref[rows(qa), :]], axis=1)
        k_ext = jnp.concatenate(
            [k_ref[rows(ka), :], jnp.where(key_lane == lane_a, MASK_BIG, 0.0).astype(BF16)], axis=1)
        pos = pos_ref[pl.ds(pl.multiple_of(pos_a, blk), blk), :]
        s_ref[par] = _qk(q_ext, k_ext) * (scale * LOG2E) + pos

    def body(j, carry):
        for par in range(PIPE):
            iteration(PIPE * j + par, par)
        return carry

    lax.fori_loop(0, _pipeline_iters(n_steps, PIPE, PIPE) // PIPE, body, 0)

    def normalise(i, carry):
        acc = acc_all_ref[rows(i), :]
        o_ref[rows(i), :] = (acc[:, 0:HEAD_DIM] / acc[:, HEAD_DIM:2 * HEAD_DIM]).astype(o_ref.dtype)
        return carry

    lax.fori_loop(0, nkb, normalise, 0)


def moba_attention(qkv, rel_bias, batch, seq):
    n, three_d = qkv.shape
    n_heads = three_d // (3 * HEAD_DIM)
    blk = MOBA_BLOCK
    assert seq % blk == 0 and REL_MAX_DIST <= blk
    nq = seq // blk
    nkp = -(-nq // LANES) * LANES
    n_buckets = rel_bias.shape[0]
    assert nq < nkp
    steps = []
    for i in range(nq):
        steps.append((i, i, 1, blk, nkp - 1))
        if i >= 1:
            steps.append((i, i - 1, 0, 0, i - 1))
        steps.extend((i, kb, 0, 2 * blk, kb) for kb in range(i - 1))
    steps.append(steps[0])
    smem = pl.BlockSpec(memory_space=pltpu.SMEM)
    q_spec, k_spec, v_spec = _attn_specs(n_heads, seq)
    return pl.pallas_call(
        functools.partial(_moba_kernel, seq=seq, n_buckets=n_buckets, nkp=nkp, n_steps=len(steps)),
        out_shape=jax.ShapeDtypeStruct((n, n_heads * HEAD_DIM), BF16),
        grid=(batch, n_heads),
        in_specs=[smem, smem, q_spec, k_spec, v_spec],
        out_specs=q_spec,
        scratch_shapes=[pltpu.VMEM((seq, nkp), BF16),
                        pltpu.VMEM((3 * blk, blk), F32),
                        pltpu.VMEM((MOBA_PIPE, blk, blk), F32),
                        pltpu.VMEM((blk, HEAD_DIM), F32),
                        pltpu.VMEM((blk, 2 * HEAD_DIM), F32),
                        pltpu.VMEM((seq, 2 * HEAD_DIM), F32)],
        compiler_params=_params("parallel", "parallel"),
        name="moba_attention",
    )(_step_table(steps, MOBA_PIPE, MOBA_PIPE), rel_bias.astype(F32), qkv, qkv, qkv)


def kernel(x, norm_mix_pre, norm_mix_post, norm_ffn_pre, norm_ffn_post, lru_w_in, lru_b_in, lru_conv_w, lru_conv_b, lru_w_ga, lru_b_ga, lru_w_gx, lru_b_gx, lru_lambda, lru_w_out, lru_b_out, sb_w_qkv, sb_w_o, moba_w_qkv, moba_w_o, rel_bias, mlp_w_up, mlp_w_down):
    batch, seq, d = x.shape
    depth = norm_mix_pre.shape[0]
    n_mixers = 3
    xt = x.reshape(batch * seq, d)
    lru_w_in, lru_w_out, sb_w_qkv, sb_w_o, moba_w_qkv, moba_w_o, mlp_w_up, mlp_w_down = (
        w.astype(BF16) for w in (lru_w_in, lru_w_out, sb_w_qkv, sb_w_o, moba_w_qkv, moba_w_o,
                                 mlp_w_up, mlp_w_down))
    h = rmsnorm(xt, norm_mix_pre[0])
    for layer in range(depth):
        mixer = layer % n_mixers
        j = layer // n_mixers
        if mixer == 0:
            m = rglru_block(h, (lru_w_in, j), lru_b_in[j], lru_conv_w[j], lru_conv_b[j],
                            lru_w_ga[j], lru_b_ga[j], lru_w_gx[j], lru_b_gx[j],
                            lru_lambda[j], (lru_w_out, j), lru_b_out[j], batch, seq)
        elif mixer == 1:
            qkv = matmul(h, (sb_w_qkv, j), out_dtype=BF16)
            m = matmul(sb_attention(qkv, batch, seq), (sb_w_o, j))
        else:
            qkv = matmul(h, (moba_w_qkv, j), out_dtype=BF16)
            m = matmul(moba_attention(qkv, rel_bias, batch, seq), (moba_w_o, j))
        xt, h = resid_norm(xt, m, norm_mix_post[layer], norm_ffn_pre[layer])
        u = matmul(h, (mlp_w_up, layer), act="relu2", out_dtype=BF16)
        m = matmul(u, (mlp_w_down, layer))
        g_next = norm_mix_pre[layer + 1] if layer + 1 < depth else None
        xt, h = resid_norm(xt, m, norm_ffn_post[layer], g_next)
    return xt.reshape(batch, seq, d)
```

```python
import functools
import math

import jax
import jax.numpy as jnp
import numpy as np
from jax import lax
from jax.experimental import pallas as pl
from jax.experimental.pallas import tpu as pltpu

HEAD_DIM = 128
LANES = 128
MXU_WIDTH = 256
SUBLANES = 8
BF16_ROWS = 16
VMEM_LIMIT_BYTES = 56 * 1024 * 1024
NORM_EPS = 1e-6
LRU_C = 8.0
LRU_CHANNEL_TILE = 2688
MOBA_BLOCK = 256
MOBA_TOPK = 3
REL_MAX_DIST = 128
NEG_BIG = -1e30
MASK_BIG = 2.0 ** 100
SB_PIPE = 2
MOBA_PIPE = 4
SB_UNDERFLOW = -105.0
LOG2E = math.log2(math.e)

BF16 = jnp.bfloat16
F32 = jnp.float32


def _pick(dim, target, align):
    best = None
    for t in range(align, min(dim, target) + 1, align):
        if dim % t == 0:
            best = t
    assert best is not None, (dim, target, align)
    return best


def _params(*sem):
    return pltpu.CompilerParams(dimension_semantics=sem, vmem_limit_bytes=VMEM_LIMIT_BYTES)


def _rms_scale(v):
    return lax.rsqrt(jnp.mean(v * v, axis=-1, keepdims=True) + NORM_EPS)


def _rmsnorm_kernel(x_ref, g_ref, h_ref):
    x = x_ref[...]
    h_ref[...] = (x * _rms_scale(x) * g_ref[...]).astype(h_ref.dtype)


def rmsnorm(x, g):
    n, d = x.shape
    tr = _pick(n, 256, SUBLANES)
    return pl.pallas_call(
        _rmsnorm_kernel,
        out_shape=jax.ShapeDtypeStruct((n, d), BF16),
        grid=(n // tr,),
        in_specs=[pl.BlockSpec((tr, d), lambda i: (i, 0)),
                  pl.BlockSpec((1, d), lambda i: (0, 0))],
        out_specs=pl.BlockSpec((tr, d), lambda i: (i, 0)),
        compiler_params=_params("parallel"),
        name="rmsnorm",
    )(x, g.reshape(1, d))


def _resid_norm_kernel(x_ref, m_ref, gpost_ref, gnext_ref, xo_ref, ho_ref):
    m = m_ref[...]
    xn = x_ref[...] + m * _rms_scale(m) * gpost_ref[...]
    xo_ref[...] = xn
    ho_ref[...] = (xn * _rms_scale(xn) * gnext_ref[...]).astype(ho_ref.dtype)


def _resid_kernel(x_ref, m_ref, gpost_ref, xo_ref):
    m = m_ref[...]
    xo_ref[...] = x_ref[...] + m * _rms_scale(m) * gpost_ref[...]


def resid_norm(x, m, g_post, g_next):
    n, d = x.shape
    tr = _pick(n, 256, SUBLANES)
    row = pl.BlockSpec((tr, d), lambda i: (i, 0))
    vec = pl.BlockSpec((1, d), lambda i: (0, 0))
    if g_next is None:
        return pl.pallas_call(
            _resid_kernel,
            out_shape=jax.ShapeDtypeStruct((n, d), F32),
            grid=(n // tr,), in_specs=[row, row, vec], out_specs=row,
            compiler_params=_params("parallel"), name="resid",
        )(x, m, g_post.reshape(1, d)), None
    return pl.pallas_call(
        _resid_norm_kernel,
        out_shape=(jax.ShapeDtypeStruct((n, d), F32), jax.ShapeDtypeStruct((n, d), BF16)),
        grid=(n // tr,), in_specs=[row, row, vec, vec], out_specs=(row, row),
        compiler_params=_params("parallel"), name="resid_norm",
    )(x, m, g_post.reshape(1, d), g_next.reshape(1, d))


def _epilogue(r, bias_ref, act):
    if bias_ref is not None:
        r = r + bias_ref[...]
    if act == "relu2":
        r = jnp.maximum(r, 0.0)
        r = r * r
    return r


def _mm_kernel(*refs, nk, has_bias, act):
    a_ref, b_ref = refs[0], refs[1]
    bias_ref = refs[2] if has_bias else None
    o_ref = refs[2 + has_bias]
    part = jnp.dot(a_ref[...], b_ref[...], preferred_element_type=F32)
    if nk == 1:
        r = _epilogue(part, bias_ref, act).astype(o_ref.dtype)
        if len(o_ref.shape) == 3:
            for hh in range(o_ref.shape[0]):
                o_ref[hh] = r[:, hh * HEAD_DIM:(hh + 1) * HEAD_DIM]
        else:
            o_ref[...] = r
        return
    acc_ref = refs[3 + has_bias]
    k = pl.program_id(2)

    @pl.when(k == 0)
    def _():
        acc_ref[...] = part

    @pl.when((k > 0) & (k < nk - 1))
    def _():
        acc_ref[...] += part

    @pl.when(k == nk - 1)
    def _():
        o_ref[...] = _epilogue(acc_ref[...] + part, bias_ref, act).astype(o_ref.dtype)


def matmul(a, b, bias=None, act=None, out_dtype=F32, head_major=False):
    b, layer = b
    m, kdim = a.shape
    _, _, n = b.shape
    tm = _pick(m, 1024, SUBLANES)
    tn = _pick(n, 1024, MXU_WIDTH if n % MXU_WIDTH == 0 else LANES)
    tk = _pick(kdim, 4096, LANES)
    nk = kdim // tk
    in_specs = [pl.BlockSpec((tm, tk), lambda i, j, k: (i, k)),
                pl.BlockSpec((None, tk, tn), lambda i, j, k: (layer, k, j))]
    args = [a, b]
    if bias is not None:
        in_specs.append(pl.BlockSpec((1, tn), lambda i, j, k: (0, j)))
        args.append(bias.reshape(1, n).astype(F32))
    return pl.pallas_call(
        functools.partial(_mm_kernel, nk=nk, has_bias=bias is not None, act=act),
        out_shape=(jax.ShapeDtypeStruct((n // HEAD_DIM, m, HEAD_DIM), out_dtype) if head_major
                   else jax.ShapeDtypeStruct((m, n), out_dtype)),
        grid=(m // tm, n // tn, nk),
        in_specs=in_specs,
        out_specs=(pl.BlockSpec((tn // HEAD_DIM, tm, HEAD_DIM), lambda i, j, k: (j, i, 0))
                   if head_major else pl.BlockSpec((tm, tn), lambda i, j, k: (i, j))),
        scratch_shapes=[pltpu.VMEM((tm, tn), F32)] if nk > 1 else [],
        compiler_params=_params("parallel", "parallel", "arbitrary"),
        name="matmul",
    )(*args)


def _softplus(x):
    return jnp.maximum(x, 0.0) + jnp.log1p(jnp.exp(-jnp.abs(x)))


def _neg_expm1(x, exp_x):
    series = (((x * (1.0 / 24) + (1.0 / 6)) * x + 0.5) * x + 1.0) * (-x)
    return jnp.where(x > -1.0 / 32, series, 1.0 - exp_x)


def _lru_kernel(g_ref, x_ref, cw_ref, cb_ref, wa_ref, ba_ref, wx_ref, bx_ref, lam_ref,
                y_ref, ext_ref, a_ref, hs_ref, hcar_ref, *, ts, conv_width, windows):
    pad = SUBLANES

    @pl.when(pl.program_id(2) == 0)
    def _():
        ext_ref[0:pad, :] = jnp.zeros((pad, ext_ref.shape[1]), F32)
        hcar_ref[...] = jnp.zeros_like(hcar_ref)

    xcur = x_ref[...]
    ext_ref[pad:pad + ts, :] = xcur
    xb = cb_ref[...] + cw_ref[conv_width - 1:conv_width, :] * xcur
    for k in range(conv_width - 1):
        off = pad - (conv_width - 1) + k
        xb = xb + cw_ref[k:k + 1, :] * ext_ref[off:off + ts, :]
    ext_ref[0:pad, :] = xcur[ts - pad:ts, :]

    xb16 = xb.astype(BF16)

    def gate(w_ref, b_ref, z_ref):
        z_ref[...] = jnp.broadcast_to(b_ref[...], z_ref.shape)
        for m, (a0, wd) in enumerate(windows):
            z_ref[:, a0:a0 + wd] += jnp.dot(xb16[:, a0:a0 + wd], w_ref[m, 0:wd, 0:wd],
                                            preferred_element_type=F32)
        return jax.nn.sigmoid(z_ref[...])

    r = gate(wa_ref, ba_ref, a_ref)
    i = gate(wx_ref, bx_ref, hs_ref)
    log_a = (-LRU_C * r) * _softplus(-lam_ref[...])
    a = jnp.exp(log_a)
    a_ref[...] = a
    var = _neg_expm1(2.0 * log_a, a * a)
    hs_ref[...] = (var * lax.rsqrt(jnp.maximum(var, 1e-30))) * (i * xb)

    def step(t, h):
        h = a_ref[pl.ds(t, 1), :] * h + hs_ref[pl.ds(t, 1), :]
        hs_ref[pl.ds(t, 1), :] = h
        return h

    hcar_ref[0:1, :] = lax.fori_loop(0, ts, step, hcar_ref[0:1, :], unroll=8)
    y_ref[...] = (hs_ref[...] * jax.nn.gelu(g_ref[...], approximate=True)).astype(y_ref.dtype)


def _gate_windows(bw, nbs):
    starts = [bw * m // LANES * LANES for m in range(nbs)]
    ends = [-(-bw * (m + 1) // LANES) * LANES for m in range(nbs)]
    windows = tuple((a0, b0 - a0) for a0, b0 in zip(starts, ends))
    offsets = [bw * m - a0 for m, a0 in enumerate(starts)]
    return windows, offsets


def _window_pad(w, offsets, wmax):
    nblk, bw, _ = w.shape
    nbs = len(offsets)
    parts = [jnp.pad(w[m::nbs], ((0, 0), (o, wmax - o - bw), (o, wmax - o - bw)))
             for m, o in enumerate(offsets)]
    return jnp.stack(parts, axis=1).reshape(nblk, wmax, wmax).astype(BF16)


def lru_core(u, conv_w, conv_b, w_ga, b_ga, w_gx, b_gx, lam, batch, seq):
    n, two_r = u.shape
    r = two_r // 2
    nblk, bw, _ = w_ga.shape
    ts = _pick(seq, 256, SUBLANES)
    nts = seq // ts
    cw = _pick(r, LRU_CHANNEL_TILE, bw * LANES // math.gcd(bw, LANES))
    nbs, ncw = cw // bw, r // cw
    windows, offsets = _gate_windows(bw, nbs)
    wmax = max(wd for _, wd in windows)
    conv_width = conv_w.shape[0]
    assert conv_width - 1 <= SUBLANES and nblk * bw == r
    tok = lambda b, c, s: (b * nts + s, c)
    vec = pl.BlockSpec((1, cw), lambda b, c, s: (0, c))
    wblk = pl.BlockSpec((nbs, wmax, wmax), lambda b, c, s: (c, 0, 0))
    row = lambda v: v.reshape(1, r)
    return pl.pallas_call(
        functools.partial(_lru_kernel, ts=ts, conv_width=conv_width, windows=windows),
        out_shape=jax.ShapeDtypeStruct((n, r), BF16),
        grid=(batch, ncw, nts),
        in_specs=[pl.BlockSpec((ts, cw), tok),
                  pl.BlockSpec((ts, cw), lambda b, c, s: (b * nts + s, ncw + c)),
                  pl.BlockSpec((conv_width, cw), lambda b, c, s: (0, c)),
                  vec, wblk, vec, wblk, vec, vec],
        out_specs=pl.BlockSpec((ts, cw), tok),
        scratch_shapes=[pltpu.VMEM((ts + SUBLANES, cw), F32),
                        pltpu.VMEM((ts, cw), F32),
                        pltpu.VMEM((ts, cw), F32),
                        pltpu.VMEM((SUBLANES, cw), F32)],
        compiler_params=_params("parallel", "parallel", "arbitrary"),
        name="lru_core",
    )(u, u, conv_w, row(conv_b), _window_pad(w_ga, offsets, wmax), row(b_ga),
      _window_pad(w_gx, offsets, wmax), row(b_gx), row(lam))


def rglru_block(h, w_in, b_in, conv_w, conv_b, w_ga, b_ga, w_gx, b_gx, lam, w_out, b_out,
                batch, seq):
    u = matmul(h, w_in, bias=b_in)
    y = lru_core(u, conv_w, conv_b, w_ga, b_ga, w_gx, b_gx, lam, batch, seq)
    return matmul(y, w_out, bias=b_out)


def _qk(q, k):
    return lax.dot_general(q, k, (((1,), (1,)), ((), ())), preferred_element_type=F32)


def _pipeline_iters(n_steps, lead, pipe):
    return -(-(n_steps + lead) // pipe) * pipe


def _step_table(rows, lead, pipe):
    n = len(rows)
    idx = np.clip(np.arange(_pipeline_iters(n, lead, pipe) + lead) - lead, 0, n - 1)
    return jnp.asarray(np.asarray(rows, np.int32)[idx].T)


def _sb_kernel(tab_ref, q_ref, k_ref, v_ref, o_ref, z_ref, e_ref, r_ref, acc_ref,
               upper_ref, keep_ref, pen_ref, r_all_ref, acc_all_ref, *, tq, n_steps, nq):
    scale = HEAD_DIM ** -0.5
    row = lax.broadcasted_iota(jnp.int32, (tq, tq), 0)
    col = lax.broadcasted_iota(jnp.int32, (tq, tq), 1)
    upper = (row > col).astype(BF16)
    upper_ref[0:tq, :] = upper
    upper_ref[tq:2 * tq, :] = upper
    strict = col < row
    keep_ref[0:tq, :] = jnp.ones((tq, tq), F32)
    keep_ref[tq:2 * tq, :] = jnp.where(strict, 1.0, 0.0)
    pen_ref[0:tq, :] = jnp.zeros((tq, tq), F32)
    pen_ref[tq:2 * tq, :] = jnp.where(strict, 0.0, NEG_BIG)
    z_ref[...] = jnp.zeros_like(z_ref)
    e_ref[...] = jnp.zeros_like(e_ref)
    r_ref[...] = jnp.zeros_like(r_ref)
    acc_ref[...] = jnp.zeros_like(acc_ref)

    def rows(t):
        return pl.ds(pl.multiple_of(t * tq, tq), tq)

    PIPE = SB_PIPE

    def iteration(it, par):
        qc, kc, first_c = tab_ref[0, it], tab_ref[1, it], tab_ref[2, it]
        w = jnp.exp(e_ref[par]).astype(BF16)
        acc = jnp.where(first_c == 1, 0.0, acc_ref[...]) + jnp.dot(
            w, v_ref[rows(kc), :], preferred_element_type=F32)
        acc_ref[...] = acc
        acc_all_ref[rows(qc), :] = acc
        o_ref[rows(qc), :] = acc.astype(o_ref.dtype)
        qb, first_b = tab_ref[0, it + PIPE], tab_ref[2, it + PIPE]
        diag = pl.ds(pl.multiple_of(first_b * tq, tq), tq)
        z = z_ref[par]
        sp = jnp.log(1.0 + jnp.exp(-jnp.abs(z)))
        log_beta = jnp.minimum(z, 0.0) - sp
        log_keep = (log_beta - z) * keep_ref[diag, :]
        hi = log_keep.astype(BF16)
        lo = (log_keep - hi.astype(F32)).astype(BF16)
        later = jnp.dot(jnp.concatenate([hi, lo], axis=1), upper_ref[...],
                        preferred_element_type=F32)
        r_in = jnp.where(first_b == 1, 0.0, r_ref[...])
        e_ref[par] = log_beta + later + r_in + pen_ref[diag, :]
        r_out = r_in + jnp.sum(log_keep, axis=1, keepdims=True)
        r_ref[...] = r_out
        r_all_ref[rows(qb), :] = r_out
        qa, ka = tab_ref[0, it + 2 * PIPE], tab_ref[1, it + 2 * PIPE]
        z_ref[par] = _qk(q_ref[rows(qa), :], k_ref[rows(ka), :]) * scale

    def body(j, carry):
        for par in range(PIPE):
            iteration(PIPE * j + par, par)
        return carry

    lax.fori_loop(0, _pipeline_iters(n_steps, 2 * PIPE, PIPE) // PIPE, body, 0)

    def alive(r):
        return jnp.max(r) > SB_UNDERFLOW

    def older_tiles(i, carry):
        r0 = r_all_ref[rows(i), :]

        @pl.when(alive(r0))
        def _():
            q = q_ref[rows(i), :]

            def more(c):
                return (c[0] >= 0) & alive(c[1])

            def tile(c):
                kb, r_in, acc = c
                z = _qk(q, k_ref[rows(kb), :]) * scale
                sp = jnp.log(1.0 + jnp.exp(-jnp.abs(z)))
                log_beta = jnp.minimum(z, 0.0) - sp
                log_keep = log_beta - z
                hi = log_keep.astype(BF16)
                lo = (log_keep - hi.astype(F32)).astype(BF16)
                later = jnp.dot(jnp.concatenate([hi, lo], axis=1), upper_ref[...],
                                preferred_element_type=F32)
                w = jnp.exp(log_beta + later + r_in).astype(BF16)
                acc = acc + jnp.dot(w, v_ref[rows(kb), :], preferred_element_type=F32)
                return kb - 1, r_in + jnp.sum(log_keep, axis=1, keepdims=True), acc

            c = lax.while_loop(more, tile, (i - 2, r0, acc_all_ref[rows(i), :]))
            o_ref[rows(i), :] = c[2].astype(o_ref.dtype)

        return carry

    if nq > 2:
        @pl.when(alive(r_all_ref[2 * tq:nq * tq, :]))
        def _():
            lax.fori_loop(2, nq, older_tiles, 0)


def _attn_specs(n_heads, seq):
    q_spec = pl.BlockSpec((None, seq, HEAD_DIM), lambda b, h: (h, b, 0))
    k_spec = pl.BlockSpec((None, seq, HEAD_DIM), lambda b, h: (n_heads + h, b, 0))
    v_spec = pl.BlockSpec((None, seq, HEAD_DIM), lambda b, h: (2 * n_heads + h, b, 0))
    return q_spec, k_spec, v_spec


def sb_attention(qkv, batch, seq):
    three_h, n, _ = qkv.shape
    n_heads = three_h // 3
    tq = _pick(seq, 256, LANES)
    nq = seq // tq
    steps = [(i, kb, int(kb == i)) for i in range(nq) for kb in range(i, max(i - 2, -1), -1)]
    steps.append(steps[0])
    q_spec, k_spec, v_spec = _attn_specs(n_heads, seq)
    return pl.pallas_call(
        functools.partial(_sb_kernel, tq=tq, n_steps=len(steps), nq=nq),
        out_shape=jax.ShapeDtypeStruct((n, n_heads * HEAD_DIM), BF16),
        grid=(batch, n_heads),
        in_specs=[pl.BlockSpec(memory_space=pltpu.SMEM), q_spec, k_spec, v_spec],
        out_specs=pl.BlockSpec((seq, HEAD_DIM), lambda b, h: (b, h)),
        scratch_shapes=[pltpu.VMEM((SB_PIPE, tq, tq), F32),
                        pltpu.VMEM((SB_PIPE, tq, tq), F32),
                        pltpu.VMEM((tq, 1), F32),
                        pltpu.VMEM((tq, HEAD_DIM), F32),
                        pltpu.VMEM((2 * tq, tq), BF16),
                        pltpu.VMEM((2 * tq, tq), F32),
                        pltpu.VMEM((2 * tq, tq), F32),
                        pltpu.VMEM((seq, 1), F32),
                        pltpu.VMEM((seq, HEAD_DIM), F32)],
        compiler_params=_params("parallel", "parallel"),
        name="sb_attention",
    )(_step_table(steps, 2 * SB_PIPE, SB_PIPE), qkv, qkv, qkv)


def _t5_bucket(dist, n_buckets):
    max_exact = n_buckets // 2
    n_f = jnp.maximum(dist, 1).astype(F32)
    large = max_exact + (jnp.log(n_f / max_exact) / math.log(REL_MAX_DIST / max_exact)
                         * (n_buckets - max_exact)).astype(jnp.int32)
    large = jnp.minimum(large, n_buckets - 1)
    return jnp.where(dist < max_exact, dist, large)


def _moba_kernel(tab_ref, bias_ref, q_ref, k_ref, v_ref, o_ref,
                 sel_ref, pos_ref, s_ref, m_ref, acc_ref, acc_all_ref,
                 *, seq, n_buckets, nkp, n_steps):
    blk = MOBA_BLOCK
    h = pl.program_id(1)
    nkb = seq // blk
    nkr = -(-nkb // BF16_ROWS) * BF16_ROWS
    scale = HEAD_DIM ** -0.5

    r = lax.broadcasted_iota(jnp.int32, (nkr, seq), 0)
    c = lax.broadcasted_iota(jnp.int32, (nkr, seq), 1)
    lo_edge = r * blk
    avg = jnp.where((c >= lo_edge) & (c < lo_edge + blk), 1.0 / blk, 0.0).astype(BF16)
    km = jnp.dot(avg, k_ref[...], preferred_element_type=F32)
    km_hi = km.astype(BF16)
    km_lo = (km - km_hi.astype(F32)).astype(BF16)

    q_all = q_ref[...]
    gate = _qk(km_hi, q_all) + _qk(km_lo, q_all)
    neg_inf = jnp.float32(-jnp.inf)
    g = jnp.where(lo_edge + blk <= c, gate, neg_inf)
    r_f = r.astype(F32)
    sel = jnp.zeros((nkr, seq), F32)
    for _ in range(max(1, min(MOBA_TOPK, nkb - 1))):
        mx = jnp.max(g, axis=0, keepdims=True)
        first = jnp.min(jnp.where(g == mx, r_f, float(nkr)), axis=0, keepdims=True)
        pick = (r_f == first) & (mx > neg_inf)
        sel = jnp.where(pick, 1.0, sel)
        g = jnp.where(pick, neg_inf, g)
    sel_t = (sel - 1.0).astype(BF16)
    row = lax.broadcasted_iota(jnp.int32, (blk, blk), 0)
    col = lax.broadcasted_iota(jnp.int32, (blk, blk), 1)
    eye = (row == col).astype(BF16)
    pad = jnp.zeros((nkp - nkr, blk), BF16)
    for i in range(nkb):
        cols = jnp.concatenate([sel_t[:, i * blk:(i + 1) * blk], pad], axis=0)
        sel_ref[i * blk:(i + 1) * blk, :] = _qk(eye, cols).astype(BF16)

    mcol = lax.broadcasted_iota(jnp.int32, (SUBLANES, 2 * blk), 1)
    dist = jnp.where(mcol <= blk, blk - mcol, 3 * blk - mcol)
    bucket = _t5_bucket(dist, n_buckets)
    vals = jnp.zeros((SUBLANES, 2 * blk), F32)
    for bk in range(n_buckets):
        vals = jnp.where(bucket == bk, bias_ref[bk, h], vals)
    tile = pltpu.roll(jnp.broadcast_to(vals[0:1, :], (blk, 2 * blk)), 0, 1, stride=1, stride_axis=0)
    tile = tile * LOG2E
    pos_ref[0:blk, :] = tile[:, 0:blk]
    pos_ref[blk:2 * blk, :] = jnp.where(col <= row, tile[:, blk:2 * blk], NEG_BIG)
    pos_ref[2 * blk:3 * blk, :] = jnp.full((blk, blk), bias_ref[n_buckets - 1, h] * LOG2E, F32)

    def rows(t):
        return pl.ds(pl.multiple_of(t * blk, blk), blk)

    s_ref[...] = jnp.zeros_like(s_ref)
    m_ref[...] = jnp.zeros_like(m_ref)
    acc_ref[...] = jnp.zeros_like(acc_ref)
    ones = jnp.ones((blk, HEAD_DIM), BF16)
    key_lane = lax.broadcasted_iota(jnp.int32, (blk, nkp), 1)

    PIPE = MOBA_PIPE

    def iteration(it, par):
        qb, kb, own_b = tab_ref[0, it], tab_ref[1, it], tab_ref[2, it]
        s = s_ref[par]
        m_old = jnp.where(own_b == 1, NEG_BIG, m_ref[...])
        m_new = jnp.maximum(m_old, jnp.broadcast_to(jnp.max(s, axis=1, keepdims=True),
                                                    (blk, HEAD_DIM)))
        p = jnp.exp2(s - jnp.concatenate([m_new, m_new], axis=1))
        alpha = jnp.exp2(m_old - m_new)
        v_ext = jnp.concatenate([v_ref[rows(kb), :], ones], axis=1)
        acc = (jnp.concatenate([alpha, alpha], axis=1) * acc_ref[...]
               + jnp.dot(p.astype(BF16), v_ext, preferred_element_type=F32))
        m_ref[...] = m_new
        acc_ref[...] = acc
        acc_all_ref[rows(qb), :] = acc
        qa, ka, pos_a, lane_a = (tab_ref[0, it + PIPE], tab_ref[1, it + PIPE],
                                 tab_ref[3, it + PIPE], tab_ref[4, it + PIPE])
        q_ext = jnp.concatenate([q_ref[rows(qa), :], sel_ref[rows(qa), :]], axis=1)
        k_ext = jnp.concatenate(
            [k_ref[rows(ka), :], jnp.where(key_lane == lane_a, MASK_BIG, 0.0).astype(BF16)], axis=1)
        pos = pos_ref[pl.ds(pl.multiple_of(pos_a, blk), blk), :]
        s_ref[par] = _qk(q_ext, k_ext) * (scale * LOG2E) + pos

    def body(j, carry):
        for par in range(PIPE):
            iteration(PIPE * j + par, par)
        return carry

    lax.fori_loop(0, _pipeline_iters(n_steps, PIPE, PIPE) // PIPE, body, 0)

    def normalise(i, carry):
        acc = acc_all_ref[rows(i), :]
        o_ref[rows(i), :] = (acc[:, 0:HEAD_DIM] / acc[:, HEAD_DIM:2 * HEAD_DIM]).astype(o_ref.dtype)
        return carry

    lax.fori_loop(0, nkb, normalise, 0)


def moba_attention(qkv, rel_bias, batch, seq):
    three_h, n, _ = qkv.shape
    n_heads = three_h // 3
    blk = MOBA_BLOCK
    assert seq % blk == 0 and REL_MAX_DIST <= blk
    nq = seq // blk
    nkp = -(-nq // LANES) * LANES
    n_buckets = rel_bias.shape[0]
    assert nq < nkp
    steps = []
    for i in range(nq):
        steps.append((i, i, 1, blk, nkp - 1))
        if i >= 1:
            steps.append((i, i - 1, 0, 0, i - 1))
        steps.extend((i, kb, 0, 2 * blk, kb) for kb in range(i - 1))
    steps.append(steps[0])
    smem = pl.BlockSpec(memory_space=pltpu.SMEM)
    q_spec, k_spec, v_spec = _attn_specs(n_heads, seq)
    return pl.pallas_call(
        functools.partial(_moba_kernel, seq=seq, n_buckets=n_buckets, nkp=nkp, n_steps=len(steps)),
        out_shape=jax.ShapeDtypeStruct((n, n_heads * HEAD_DIM), BF16),
        grid=(batch, n_heads),
        in_specs=[smem, smem, q_spec, k_spec, v_spec],
        out_specs=pl.BlockSpec((seq, HEAD_DIM), lambda b, h: (b, h)),
        scratch_shapes=[pltpu.VMEM((seq, nkp), BF16),
                        pltpu.VMEM((3 * blk, blk), F32),
                        pltpu.VMEM((MOBA_PIPE, blk, blk), F32),
                        pltpu.VMEM((blk, HEAD_DIM), F32),
                        pltpu.VMEM((blk, 2 * HEAD_DIM), F32),
                        pltpu.VMEM((seq, 2 * HEAD_DIM), F32)],
        compiler_params=_params("parallel", "parallel"),
        name="moba_attention",
    )(_step_table(steps, MOBA_PIPE, MOBA_PIPE), rel_bias.astype(F32), qkv, qkv, qkv)


def kernel(x, norm_mix_pre, norm_mix_post, norm_ffn_pre, norm_ffn_post, lru_w_in, lru_b_in, lru_conv_w, lru_conv_b, lru_w_ga, lru_b_ga, lru_w_gx, lru_b_gx, lru_lambda, lru_w_out, lru_b_out, sb_w_qkv, sb_w_o, moba_w_qkv, moba_w_o, rel_bias, mlp_w_up, mlp_w_down):
    batch, seq, d = x.shape
    depth = norm_mix_pre.shape[0]
    n_mixers = 3
    xt = x.reshape(batch * seq, d)
    lru_w_in, lru_w_out, sb_w_qkv, sb_w_o, moba_w_qkv, moba_w_o, mlp_w_up, mlp_w_down = (
        w.astype(BF16) for w in (lru_w_in, lru_w_out, sb_w_qkv, sb_w_o, moba_w_qkv, moba_w_o,
                                 mlp_w_up, mlp_w_down))
    h = rmsnorm(xt, norm_mix_pre[0])
    for layer in range(depth):
        mixer = layer % n_mixers
        j = layer // n_mixers
        if mixer == 0:
            m = rglru_block(h, (lru_w_in, j), lru_b_in[j], lru_conv_w[j], lru_conv_b[j],
                            lru_w_ga[j], lru_b_ga[j], lru_w_gx[j], lru_b_gx[j],
                            lru_lambda[j], (lru_w_out, j), lru_b_out[j], batch, seq)
        elif mixer == 1:
            qkv = matmul(h, (sb_w_qkv, j), out_dtype=BF16, head_major=True)
            m = matmul(sb_attention(qkv, batch, seq), (sb_w_o, j))
        else:
            qkv = matmul(h, (moba_w_qkv, j), out_dtype=BF16, head_major=True)
            m = matmul(moba_attention(qkv, rel_bias, batch, seq), (moba_w_o, j))
        xt, h = resid_norm(xt, m, norm_mix_post[layer], norm_ffn_pre[layer])
        u = matmul(h, (mlp_w_up, layer), act="relu2", out_dtype=BF16)
        m = matmul(u, (mlp_w_down, layer))
        g_next = norm_mix_pre[layer + 1] if layer + 1 < depth else None
        xt, h = resid_norm(xt, m, norm_ffn_post[layer], g_next)
    return xt.reshape(batch, seq, d)
```
